```python
import jax, jax.numpy as jnp
from jax import lax
import numpy as np

D_MODEL = 1024
BATCH = 4
SEQ = 8192
DEPTH = 4

MEM_LEN = 256

POOL_W = D_MODEL // 4
NSA_W = D_MODEL // 4
SGU_W = D_MODEL // 4
CONV_W = D_MODEL - POOL_W - NSA_W - SGU_W
D_MIX = POOL_W + NSA_W + SGU_W + CONV_W

POOL_GROUPS = 4
POOL_WINDOWS = (2, 4, 8, 16)
POOL_GW = POOL_W // POOL_GROUPS

NSA_HEADS = 4
NSA_DH = NSA_W // NSA_HEADS
CMP_BLOCK = 32
CMP_STRIDE = 16
CMP_HIDDEN = 256
SEL_BLOCK = 64
SEL_TOPK = 16
WINDOW = 512
Q_BLOCK = 128
FORCE_SCORE = 1e4

SGU_GROUPS = 4
SGU_GW = SGU_W // SGU_GROUPS
SGU_CHUNK = 128

CONV_WIDTH = 31

X_HEADS = 4
X_DH = D_MODEL // X_HEADS

D_FF = 2816
N_EXPERTS = 8
TOP_K = 2
D_FF_EXPERT = 3584
MOE_BLOCK = 512
N_DENSE = (DEPTH + 1) // 2
N_MOE = DEPTH // 2

ALPHA = (2 * DEPTH) ** 0.25
BETA = (8 * DEPTH) ** -0.25
LN_EPS = 1e-5
NEG_INF = -1e30

IN_SPLITS = (POOL_W, NSA_W, 6 * NSA_DH, 3 * NSA_HEADS, 2 * SGU_W, 2 * CONV_W)
D_IN = sum(IN_SPLITS)
IN_OFFSETS = tuple(sum(IN_SPLITS[:i + 1]) for i in range(len(IN_SPLITS) - 1))

kernel_name = "hybrid_pool_nsa_sgu_conv_moe_trunk"


def layer_norm(x, g, b):
    xf = x.astype(jnp.float32)
    mu = jnp.mean(xf, axis=-1, keepdims=True)
    var = jnp.mean(jnp.square(xf - mu), axis=-1, keepdims=True)
    y = (xf - mu) * lax.rsqrt(var + LN_EPS)
    return (y * g.astype(jnp.float32) + b.astype(jnp.float32)).astype(x.dtype)


def masked_softmax(s, mask):
    p = jax.nn.softmax(jnp.where(mask, s, NEG_INF), axis=-1)
    return jnp.where(mask, p, 0.0)


def pool_mixer(a, w, scale):
    b, t, _ = a.shape
    af = a.astype(jnp.float32).reshape(b, t, POOL_GROUPS, POOL_GW)
    cs = jnp.cumsum(af, axis=1)
    pos = jnp.arange(t)
    outs = []
    for g, win in enumerate(POOL_WINDOWS):
        c = cs[:, :, g]
        prev = jnp.pad(c, ((0, 0), (win, 0), (0, 0)))[:, :t]
        cnt = jnp.minimum(pos + 1, win).astype(jnp.float32)[None, :, None]
        outs.append((c - prev) / cnt - af[:, :, g])
    d = jnp.stack(outs, axis=2).astype(a.dtype)
    y = jnp.einsum('btgc,gcd->btgd', d, w).reshape(b, t, POOL_W)
    return y * scale


def compress_kv(raw, pe, w1, w2):
    b, t, dk = raw.shape
    n_chunk = t // CMP_STRIDE
    per = CMP_BLOCK // CMP_STRIDE
    nc = n_chunk - per + 1
    c = raw.reshape(b, n_chunk, CMP_STRIDE, dk)
    blocks = jnp.concatenate([c[:, p:p + nc] for p in range(per)], axis=2) + pe
    h = jax.nn.gelu(blocks.reshape(b, nc, CMP_BLOCK * dk) @ w1)
    return h @ w2


def nsa_mixer(q, kv, gate_logits, pe_k, wk1, wk2, pe_v, wv1, wv2):
    b, t, _ = q.shape
    qh = q.reshape(b, t, NSA_HEADS, NSA_DH)
    k_cmp, v_cmp, k_sel, v_sel, k_win, v_win = jnp.split(kv, 6, axis=-1)
    kc = compress_kv(k_cmp, pe_k, wk1, wk2)
    vc = compress_kv(v_cmp, pe_v, wv1, wv2)
    nc = kc.shape[1]
    ns = t // SEL_BLOCK
    n_sel = min(SEL_TOPK, ns)
    ks_blocks = k_sel.reshape(b, ns, SEL_BLOCK, NSA_DH)
    vs_blocks = v_sel.reshape(b, ns, SEL_BLOCK, NSA_DH)
    kw_pad = jnp.pad(k_win, ((0, 0), (WINDOW, 0), (0, 0)))
    vw_pad = jnp.pad(v_win, ((0, 0), (WINDOW, 0), (0, 0)))
    gates = jax.nn.sigmoid(gate_logits.astype(jnp.float32)).reshape(b, t, NSA_HEADS, 3)
    scale = NSA_DH ** -0.5
    cmp_end = jnp.arange(nc) * CMP_STRIDE + CMP_BLOCK - 1
    sel_ids = jnp.arange(ns)
    ratio = SEL_BLOCK // CMP_STRIDE
    lead = CMP_BLOCK // CMP_STRIDE - 1
    tail = ratio * ns - nc

    def query_block(i):
        q0 = i * Q_BLOCK
        tq = q0 + jnp.arange(Q_BLOCK)
        qb = lax.dynamic_slice_in_dim(qh, q0, Q_BLOCK, axis=1)
        s = jnp.einsum('bqhd,bnd->bhqn', qb, kc).astype(jnp.float32) * scale
        p_c = masked_softmax(s, cmp_end[None, :] <= tq[:, None])
        o_c = jnp.einsum('bhqn,bnd->bqhd', p_c.astype(vc.dtype), vc)
        imp = jnp.pad(p_c.sum(axis=1), ((0, 0), (0, 0), (lead, tail)))
        imp_sel = imp[..., :ratio * ns].reshape(b, Q_BLOCK, ns, ratio).sum(-1)
        for p in range(lead):
            imp_sel = imp_sel + imp[..., ratio + p:ratio + p + ratio * ns:ratio]
        cur = tq // SEL_BLOCK
        valid = sel_ids[None, :] <= cur[:, None]
        forced = ((sel_ids[None, :] == 0) | (sel_ids[None, :] == cur[:, None])
                  | (sel_ids[None, :] == cur[:, None] - 1))
        score = jnp.where(valid, imp_sel + FORCE_SCORE * forced, -1.0)
        top_s, idx = lax.top_k(score, n_sel)
        ks = jax.vmap(lambda kb, ib: kb[ib])(ks_blocks, idx)
        vs = jax.vmap(lambda vb, ib: vb[ib])(vs_blocks, idx)
        kpos = idx[..., None] * SEL_BLOCK + jnp.arange(SEL_BLOCK)
        m_s = (top_s >= 0.0)[..., None] & (kpos <= tq[None, :, None, None])
        s = jnp.einsum('bqhd,bqnkd->bhqnk', qb, ks).astype(jnp.float32) * scale
        s = s.reshape(b, NSA_HEADS, Q_BLOCK, n_sel * SEL_BLOCK)
        m_s = m_s.reshape(b, 1, Q_BLOCK, n_sel * SEL_BLOCK)
        p_s = masked_softmax(s, m_s).reshape(b, NSA_HEADS, Q_BLOCK, n_sel, SEL_BLOCK)
        o_s = jnp.einsum('bhqnk,bqnkd->bqhd', p_s.astype(vs.dtype), vs)
        kw = lax.dynamic_slice_in_dim(kw_pad, q0, WINDOW + Q_BLOCK, axis=1)
        vw = lax.dynamic_slice_in_dim(vw_pad, q0, WINDOW + Q_BLOCK, axis=1)
        spos = q0 - WINDOW + jnp.arange(WINDOW + Q_BLOCK)
        m_w = ((spos[None, :] >= 0) & (spos[None, :] <= tq[:, None])
               & (spos[None, :] > tq[:, None] - WINDOW))
        s = jnp.einsum('bqhd,bkd->bhqk', qb, kw).astype(jnp.float32) * scale
        p_w = masked_softmax(s, m_w)
        o_w = jnp.einsum('bhqk,bkd->bqhd', p_w.astype(vw.dtype), vw)
        g = lax.dynamic_slice_in_dim(gates, q0, Q_BLOCK, axis=1)
        o = g[..., 0:1] * o_c + g[..., 1:2] * o_s + g[..., 2:3] * o_w
        return o.astype(qh.dtype)

    out = lax.map(query_block, jnp.arange(t // Q_BLOCK))
    return out.transpose(1, 0, 2, 3, 4).reshape(b, t, NSA_W)


def sgu_mixer(uv, ln_g, ln_b, w_s, b_s):
    b, t, _ = uv.shape
    u, v = jnp.split(jax.nn.gelu(uv), 2, axis=-1)
    v = layer_norm(v, ln_g, ln_b)
    v = v.reshape(b, t // SGU_CHUNK, SGU_CHUNK, SGU_GROUPS, SGU_GW)
    causal = jnp.tril(jnp.ones((SGU_CHUNK, SGU_CHUNK), dtype=bool))
    w = jnp.where(causal, w_s, 0.0)
    mix = jnp.einsum('gij,bnjgc->bnigc', w, v) + b_s.T[None, None, :, :, None]
    return u * mix.reshape(b, t, SGU_W)


def conv_mixer(ag, w_dw, b_dw, ln_g, ln_b, w_pw):
    a, g = jnp.split(ag, 2, axis=-1)
    h = a * jax.nn.sigmoid(g)
    h = lax.conv_general_dilated(
        h, w_dw[:, None, :], window_strides=(1,), padding=[(CONV_WIDTH - 1, 0)],
        dimension_numbers=("NWC", "WIO", "NWC"), feature_group_count=CONV_W) + b_dw
    h = jax.nn.silu(layer_norm(h, ln_g, ln_b))
    return h @ w_pw


def hybrid_mixer(h, w_in, pool_w, pool_scale, pe_k, wk1, wk2, pe_v, wv1, wv2,
                 sgu_ln_g, sgu_ln_b, sgu_w, sgu_b, conv_w, conv_b, conv_ln_g,
                 conv_ln_b, conv_pw, w_out):
    z = h @ w_in
    a, q, kv, gl, uv, ag = jnp.split(z, IN_OFFSETS, axis=-1)
    y = jnp.concatenate([
        pool_mixer(a, pool_w, pool_scale),
        nsa_mixer(q, kv, gl, pe_k, wk1, wk2, pe_v, wv1, wv2),
        sgu_mixer(uv, sgu_ln_g, sgu_ln_b, sgu_w, sgu_b),
        conv_mixer(ag, conv_w, conv_b, conv_ln_g, conv_ln_b, conv_pw),
    ], axis=-1)
    return y @ w_out


def memory_cross_attention(h, mem, wq, wkv, wo):
    b, t, d = h.shape
    q = (h @ wq).reshape(b, t, X_HEADS, X_DH)
    k, v = jnp.split(mem @ wkv, 2, axis=-1)
    k = k.reshape(b, -1, X_HEADS, X_DH)
    v = v.reshape(b, -1, X_HEADS, X_DH)
    s = jnp.einsum('bthd,bmhd->bhtm', q, k).astype(jnp.float32) * (X_DH ** -0.5)
    p = jax.nn.softmax(s, axis=-1)
    o = jnp.einsum('bhtm,bmhd->bthd', p.astype(v.dtype), v).reshape(b, t, d)
    return o @ wo


def swiglu(h, w13, w2):
    g, u = jnp.split(h @ w13, 2, axis=-1)
    return (jax.nn.silu(g) * u) @ w2


def moe_swiglu(h, router_w, w13, w2):
    n, d = h.shape
    logits = (h @ router_w).astype(jnp.float32)
    top_v, top_e = lax.top_k(logits, TOP_K)
    gate = jax.nn.softmax(top_v, axis=-1)
    flat_e = top_e.reshape(-1)
    flat_tok = jnp.arange(n * TOP_K, dtype=jnp.int32) // TOP_K
    order = jnp.argsort(flat_e)
    se = flat_e[order]
    counts = jnp.bincount(flat_e, length=N_EXPERTS)
    start = jnp.cumsum(counts) - counts
    pcounts = (counts + MOE_BLOCK - 1) // MOE_BLOCK * MOE_BLOCK
    pend = jnp.cumsum(pcounts)
    pstart = pend - pcounts
    dest = pstart[se] + jnp.arange(n * TOP_K) - start[se]
    m_pad = -(-(n * TOP_K) // MOE_BLOCK) * MOE_BLOCK + N_EXPERTS * MOE_BLOCK
    tok = jnp.full((m_pad,), n, jnp.int32).at[dest].set(flat_tok[order])
    gw = jnp.zeros((m_pad,), jnp.float32).at[dest].set(gate.reshape(-1)[order])
    n_blk = m_pad // MOE_BLOCK
    blk_e = jnp.minimum(jnp.searchsorted(pend, jnp.arange(n_blk) * MOE_BLOCK, side='right'),
                        N_EXPERTS - 1)
    h_pad = jnp.concatenate([h, jnp.zeros((1, d), h.dtype)], axis=0)

    def expert_block(args):
        tb, e = args
        g, u = jnp.split(h_pad[tb] @ w13[e], 2, axis=-1)
        return (jax.nn.silu(g) * u) @ w2[e]

    y = lax.map(expert_block, (tok.reshape(n_blk, MOE_BLOCK), blk_e))
    y = y.reshape(m_pad, d) * gw[:, None].astype(h.dtype)
    return jax.ops.segment_sum(y, tok, num_segments=n + 1)[:n]


def setup_inputs(seed: int = 0) -> dict:
    key = jax.random.key(seed)
    ks = iter(jax.random.split(key, 48))
    f32 = jnp.float32
    L, ND, NM, dk = DEPTH, N_DENSE, N_MOE, NSA_DH

    def nrm(shape, scale):
        return jax.random.normal(next(ks), shape, f32) * scale

    def gain(shape):
        return 1.0 + nrm(shape, 0.05)

    def bias(shape):
        return nrm(shape, 0.02)

    return {
        "x": nrm((BATCH, SEQ, D_MODEL), 1.0),
        "mem": nrm((BATCH, MEM_LEN, D_MODEL), 1.0),
        "w_in": nrm((L, D_MODEL, D_IN), D_MODEL ** -0.5),
        "pool_w": nrm((L, POOL_GROUPS, POOL_GW, POOL_GW), POOL_GW ** -0.5),
        "pool_scale": 1.0 + nrm((L, POOL_W), 0.1),
        "cmp_pe_k": nrm((L, CMP_BLOCK, dk), 0.1),
        "cmp_k_w1": nrm((L, CMP_BLOCK * dk, CMP_HIDDEN), (CMP_BLOCK * dk) ** -0.5),
        "cmp_k_w2": nrm((L, CMP_HIDDEN, dk), CMP_HIDDEN ** -0.5),
        "cmp_pe_v": nrm((L, CMP_BLOCK, dk), 0.1),
        "cmp_v_w1": nrm((L, CMP_BLOCK * dk, CMP_HIDDEN), (CMP_BLOCK * dk) ** -0.5),
        "cmp_v_w2": nrm((L, CMP_HIDDEN, dk), CMP_HIDDEN ** -0.5),
        "sgu_ln_g": gain((L, SGU_W)),
        "sgu_ln_b": bias((L, SGU_W)),
        "sgu_w": nrm((L, SGU_GROUPS, SGU_CHUNK, SGU_CHUNK), SGU_CHUNK ** -0.5),
        "sgu_b": 1.0 + nrm((L, SGU_GROUPS, SGU_CHUNK), 0.1),
        "conv_w": nrm((L, CONV_WIDTH, CONV_W), CONV_WIDTH ** -0.5),
        "conv_b": bias((L, CONV_W)),
        "conv_ln_g": gain((L, CONV_W)),
        "conv_ln_b": bias((L, CONV_W)),
        "conv_pw": nrm((L, CONV_W, CONV_W), CONV_W ** -0.5),
        "w_out": nrm((L, D_MIX, D_MODEL), BETA * D_MIX ** -0.5),
        "ln1_g": gain((L, D_MODEL)),
        "ln1_b": bias((L, D_MODEL)),
        "xq_w": nrm((L, D_MODEL, D_MODEL), D_MODEL ** -0.5),
        "xkv_w": nrm((L, D_MODEL, 2 * D_MODEL), D_MODEL ** -0.5),
        "xo_w": nrm((L, D_MODEL, D_MODEL), BETA * D_MODEL ** -0.5),
        "ln2_g": gain((L, D_MODEL)),
        "ln2_b": bias((L, D_MODEL)),
        "ffn_w13": nrm((ND, D_MODEL, 2 * D_FF), D_MODEL ** -0.5),
        "ffn_w2": nrm((ND, D_FF, D_MODEL), BETA * D_FF ** -0.5),
        "router_w": nrm((NM, D_MODEL, N_EXPERTS), D_MODEL ** -0.5),
        "exp_w13": nrm((NM, N_EXPERTS, D_MODEL, 2 * D_FF_EXPERT), D_MODEL ** -0.5),
        "exp_w2": nrm((NM, N_EXPERTS, D_FF_EXPERT, D_MODEL), BETA * D_FF_EXPERT ** -0.5),
        "ln3_g": gain((L, D_MODEL)),
        "ln3_b": bias((L, D_MODEL)),
    }


def reference(x, mem, w_in, pool_w, pool_scale, cmp_pe_k, cmp_k_w1, cmp_k_w2,
              cmp_pe_v, cmp_v_w1, cmp_v_w2, sgu_ln_g, sgu_ln_b, sgu_w, sgu_b,
              conv_w, conv_b, conv_ln_g, conv_ln_b, conv_pw, w_out, ln1_g, ln1_b,
              xq_w, xkv_w, xo_w, ln2_g, ln2_b, ffn_w13, ffn_w2, router_w,
              exp_w13, exp_w2, ln3_g, ln3_b):
    b, t, d = x.shape
    for l in range(DEPTH):
        mix = hybrid_mixer(x, w_in[l], pool_w[l], pool_scale[l], cmp_pe_k[l],
                           cmp_k_w1[l], cmp_k_w2[l], cmp_pe_v[l], cmp_v_w1[l],
                           cmp_v_w2[l], sgu_ln_g[l], sgu_ln_b[l], sgu_w[l], sgu_b[l],
                           conv_w[l], conv_b[l], conv_ln_g[l], conv_ln_b[l],
                           conv_pw[l], w_out[l])
        x = layer_norm(ALPHA * x + mix, ln1_g[l], ln1_b[l])
        xa = memory_cross_attention(x, mem, xq_w[l], xkv_w[l], xo_w[l])
        x = layer_norm(ALPHA * x + xa, ln2_g[l], ln2_b[l])
        if l % 2 == 0:
            f = swiglu(x, ffn_w13[l // 2], ffn_w2[l // 2])
        else:
            f = moe_swiglu(x.reshape(b * t, d), router_w[l // 2], exp_w13[l // 2],
                           exp_w2[l // 2]).reshape(b, t, d)
        x = layer_norm(ALPHA * x + f, ln3_g[l], ln3_b[l])
    return x
```

```python
import functools

import numpy as np
import jax
import jax.numpy as jnp
from jax import lax
from jax.experimental import pallas as pl
from jax.experimental.pallas import tpu as pltpu

F32 = jnp.float32
BF16 = jnp.bfloat16

POOL_GROUPS = 4
POOL_WINDOWS = (2, 4, 8, 16)
POOL_HALO = 16
NSA_HEADS = 4
NSA_DH = 64
CMP_BLOCK = 32
CMP_STRIDE = 16
SEL_BLOCK = 64
SEL_TOPK = 16
WINDOW = 512
Q_BLOCK = 128
FORCE_SCORE = 1e4
SGU_GROUPS = 4
SGU_CHUNK = 128
CONV_WIDTH = 31
CONV_HALO = 32
X_HEADS = 4
N_EXPERTS = 8
LN_EPS = 1e-5
NEG_INF = -1e30

LANES = 128
VMEM_LIMIT = 56 * 1024 * 1024
ROW_TILE = 512
SEL_KEY_TILE = 256
FFN_ROW_TILE = 1024
FFN_COL_TILE = 256
MOE_ROW_TILE = 1024
MOE_COL_TILE = 512
GATHER_ROWS = 256


def _cparams(*sem):
    return pltpu.CompilerParams(dimension_semantics=sem, vmem_limit_bytes=VMEM_LIMIT)


def _layer_norm(x, g, b):
    mu = jnp.mean(x, axis=-1, keepdims=True)
    xc = x - mu
    var = jnp.mean(xc * xc, axis=-1, keepdims=True)
    return xc * lax.rsqrt(var + LN_EPS) * g + b


def _sigmoid(x):
    return 1.0 / (1.0 + jnp.exp(-x))


def _gelu_tanh(x):
    c = np.float32(np.sqrt(2.0 / np.pi))
    return 0.5 * x * (1.0 + jnp.tanh(c * (x + np.float32(0.044715) * (x * x * x))))


def _dot(a, b):
    return jnp.dot(a, b, preferred_element_type=F32)


def _dot_nt(a, b):
    return lax.dot_general(a, b, (((1,), (1,)), ((), ())), preferred_element_type=F32)


def _iota(shape, dim):
    return lax.broadcasted_iota(jnp.int32, shape, dim)


_IN_COLS = (("ag", 512, F32), ("uv", 512, F32), ("pool", 256, F32), ("kvc", 128, F32),
            ("gl", 128, F32), ("q", 256, BF16), ("kv", 256, BF16))


def _in_proj_kernel(x_ref, w_ref, *out_refs):
    xb = x_ref[...].astype(BF16)
    off = 0
    for (_, width, dt), o_ref in zip(_IN_COLS, out_refs):
        o_ref[...] = _dot(xb, w_ref[:, off:off + width]).astype(dt)
        off += width


def _in_proj(x2, w_packed):
    n, d = x2.shape
    tm = ROW_TILE
    tot = sum(w for _, w, _ in _IN_COLS)
    return pl.pallas_call(
        _in_proj_kernel,
        grid=(n // tm,),
        in_specs=[pl.BlockSpec((tm, d), lambda i: (i, 0)),
                  pl.BlockSpec((d, tot), lambda i: (0, 0))],
        out_specs=[pl.BlockSpec((tm, w), lambda i: (i, 0)) for _, w, _ in _IN_COLS],
        out_shape=[jax.ShapeDtypeStruct((n, w), dt) for _, w, dt in _IN_COLS],
        compiler_params=_cparams("parallel"),
        name="in_proj",
    )(x2, w_packed)


def _pool_kernel(a_ref, halo_ref, w_ref, scale_ref, o_ref):
    i = pl.program_id(1)
    tt = a_ref.shape[1]
    a = a_ref[0]
    halo = jnp.where(i > 0, halo_ref[0], 0.0)
    ext = jnp.concatenate([halo, a], axis=0)
    s2 = ext + pltpu.roll(ext, 1, 0)
    s4 = s2 + pltpu.roll(s2, 2, 0)
    s8 = s4 + pltpu.roll(s4, 4, 0)
    s16 = s8 + pltpu.roll(s8, 8, 0)
    grp = _iota(a.shape, 1) // (a.shape[1] // POOL_GROUPS)
    pos = i * tt + _iota(a.shape, 0)
    h = POOL_HALO
    s = jnp.where(grp == 0, s2[h:], jnp.where(grp == 1, s4[h:], jnp.where(grp == 2, s8[h:], s16[h:])))
    win = jnp.where(grp == 0, 2, jnp.where(grp == 1, 4, jnp.where(grp == 2, 8, 16)))
    cnt = jnp.minimum(pos + 1, win).astype(F32)
    d = s / cnt - a
    y = _dot(d.astype(BF16), w_ref[...]) * scale_ref[...]
    o_ref[0] = y.astype(o_ref.dtype)


def _pool_mixer(a3, w_bd, scale):
    b, t, c = a3.shape
    tt = ROW_TILE
    hb = tt // POOL_HALO
    return pl.pallas_call(
        _pool_kernel,
        grid=(b, t // tt),
        in_specs=[pl.BlockSpec((1, tt, c), lambda bi, i: (bi, i, 0)),
                  pl.BlockSpec((1, POOL_HALO, c), lambda bi, i: (bi, jnp.maximum(i * hb - 1, 0), 0)),
                  pl.BlockSpec((c, c), lambda bi, i: (0, 0)),
                  pl.BlockSpec((1, c), lambda bi, i: (0, 0))],
        out_specs=pl.BlockSpec((1, tt, c), lambda bi, i: (bi, i, 0)),
        out_shape=jax.ShapeDtypeStruct((b, t, c), BF16),
        compiler_params=_cparams("parallel", "parallel"),
        name="pool_mixer",
    )(a3, a3, w_bd, scale)


def _conv_kernel(ag_ref, halo_ref, wdw_ref, bdw_ref, g_ref, b_ref, wpw_ref, o_ref):
    i = pl.program_id(1)
    tt = ag_ref.shape[1]
    cw = ag_ref.shape[2] // 2
    ag = jnp.concatenate([halo_ref[0], ag_ref[0]], axis=0)
    h = ag[:, :cw] * _sigmoid(ag[:, cw:])
    row = _iota(h.shape, 0)
    h = jnp.where((row >= CONV_HALO) | (i > 0), h, 0.0)
    n_ext = tt + CONV_HALO
    lead = CONV_HALO - (CONV_WIDTH - 1)
    acc = jnp.zeros((tt, cw), F32)
    for r in range(8):
        hr = h if r == 0 else pltpu.roll(h, n_ext - r, 0)
        for k in range(CONV_WIDTH):
            if (lead + k) % 8 == r:
                off = lead + k - r
                acc = acc + hr[off:off + tt] * wdw_ref[k:k + 1, :]
    y = acc + bdw_ref[...]
    y = _layer_norm(y, g_ref[...], b_ref[...])
    y = y * _sigmoid(y)
    o_ref[0] = _dot(y.astype(BF16), wpw_ref[...]).astype(o_ref.dtype)


def _conv_mixer(ag3, w_dw, b_dw, ln_g, ln_b, w_pw):
    b, t, c2 = ag3.shape
    cw = c2 // 2
    tt = ROW_TILE
    hb = tt // CONV_HALO
    full = lambda bi, i: (0, 0)
    return pl.pallas_call(
        _conv_kernel,
        grid=(b, t // tt),
        in_specs=[pl.BlockSpec((1, tt, c2), lambda bi, i: (bi, i, 0)),
                  pl.BlockSpec((1, CONV_HALO, c2), lambda bi, i: (bi, jnp.maximum(i * hb - 1, 0), 0)),
                  pl.BlockSpec((CONV_WIDTH, cw), full),
                  pl.BlockSpec((1, cw), full), pl.BlockSpec((1, cw), full), pl.BlockSpec((1, cw), full),
                  pl.BlockSpec((cw, cw), full)],
        out_specs=pl.BlockSpec((1, tt, cw), lambda bi, i: (bi, i, 0)),
        out_shape=jax.ShapeDtypeStruct((b, t, cw), BF16),
        compiler_params=_cparams("parallel", "parallel"),
        name="conv_mixer",
    )(ag3, ag3, w_dw, b_dw, ln_g, ln_b, w_pw)


def _sgu_kernel(uv_ref, g_ref, b_ref, ws_ref, bs_ref, o_ref):
    tt = uv_ref.shape[0]
    sw = uv_ref.shape[1] // 2
    gw = sw // SGU_GROUPS
    act = _gelu_tanh(uv_ref[...])
    u = act[:, :sw]
    v = _layer_norm(act[:, sw:], g_ref[...], b_ref[...]).astype(BF16)
    causal = _iota((SGU_CHUNK, SGU_CHUNK), 1) <= _iota((SGU_CHUNK, SGU_CHUNK), 0)
    ws = [jnp.where(causal, ws_ref[g], 0.0).astype(BF16) for g in range(SGU_GROUPS)]
    lane_grp = _iota((SGU_CHUNK, sw), 1) // gw
    for ch in range(tt // SGU_CHUNK):
        rows = slice(ch * SGU_CHUNK, (ch + 1) * SGU_CHUNK)
        vc = v[rows]
        mix = jnp.zeros((SGU_CHUNK, sw), F32)
        for g in range(SGU_GROUPS):
            mix = jnp.where(lane_grp == g, _dot(ws[g], vc), mix)
        o_ref[rows, :] = (u[rows] * (mix + bs_ref[...])).astype(o_ref.dtype)


def _sgu_mixer(uv2, ln_g, ln_b, w_s, bias_full):
    n, c2 = uv2.shape
    sw = c2 // 2
    tm = ROW_TILE
    return pl.pallas_call(
        _sgu_kernel,
        grid=(n // tm,),
        in_specs=[pl.BlockSpec((tm, c2), lambda i: (i, 0)),
                  pl.BlockSpec((1, sw), lambda i: (0, 0)), pl.BlockSpec((1, sw), lambda i: (0, 0)),
                  pl.BlockSpec((SGU_GROUPS, SGU_CHUNK, SGU_CHUNK), lambda i: (0, 0, 0)),
                  pl.BlockSpec((SGU_CHUNK, sw), lambda i: (0, 0))],
        out_specs=pl.BlockSpec((tm, sw), lambda i: (i, 0)),
        out_shape=jax.ShapeDtypeStruct((n, sw), BF16),
        compiler_params=_cparams("parallel"),
        name="sgu_mixer",
    )(uv2, ln_g, ln_b, w_s, bias_full)


def _compress_kernel(rc_ref, pea_ref, peb_ref, wa_ref, wb_ref, w2_ref, o_ref):
    x = rc_ref[0]
    nch = x.shape[0]
    first = _dot((x + pea_ref[...]).astype(BF16), wa_ref[...])
    second = _dot((x + peb_ref[...]).astype(BF16), wb_ref[...])
    hid = _gelu_tanh(first + pltpu.roll(second, nch - 1, 0))
    o_ref[0] = _dot(hid.astype(BF16), w2_ref[...]).astype(o_ref.dtype)


def _compress(rc3, pe_a, pe_b, w_a, w_b, w2_bd):
    b, nch, width = rc3.shape
    hid2 = w_a.shape[1]
    full = lambda bi: (0, 0)
    return pl.pallas_call(
        _compress_kernel,
        grid=(b,),
        in_specs=[pl.BlockSpec((1, nch, width), lambda bi: (bi, 0, 0)),
                  pl.BlockSpec((1, width), full), pl.BlockSpec((1, width), full),
                  pl.BlockSpec((width, hid2), full), pl.BlockSpec((width, hid2), full),
                  pl.BlockSpec((hid2, 2 * NSA_DH), full)],
        out_specs=pl.BlockSpec((1, nch, 2 * NSA_DH), lambda bi: (bi, 0, 0)),
        out_shape=jax.ShapeDtypeStruct((b, nch, 2 * NSA_DH), BF16),
        compiler_params=_cparams("parallel"),
        name="nsa_compress",
    )(rc3, pe_a, pe_b, w_a, w_b, w2_bd)


def _nsa_kernel(q_ref, gl_ref, kc_ref, kv_ref, imp_ref, o_ref, *, n_sel):
    i = pl.program_id(1)
    qb = q_ref.shape[1]
    ncp = kc_ref.shape[1]
    ns = imp_ref.shape[1]
    dh = NSA_DH
    q0 = i * qb
    q = q_ref[0]
    qh = [q[:, h * dh:(h + 1) * dh] for h in range(NSA_HEADS)]
    tq = q0 + _iota((qb, 1), 0)

    kcv = kc_ref[0]
    kc, vc = kcv[:, :dh], kcv[:, dh:]
    m_c = (_iota((qb, ncp), 1) * CMP_STRIDE + (CMP_BLOCK - 1)) <= tq
    psum = jnp.zeros((qb, ncp), F32)
    o_c = []
    for h in range(NSA_HEADS):
        s = jnp.where(m_c, _dot_nt(qh[h], kc), NEG_INF)
        p = jnp.where(m_c, jnp.exp(s - jnp.max(s, axis=-1, keepdims=True)), 0.0)
        l = jnp.sum(p, axis=-1, keepdims=True)
        p = p / jnp.where(l > 0.0, l, 1.0)
        psum = psum + p
        o_c.append(_dot(p.astype(BF16), vc))

    p_hi = psum.astype(BF16)
    r1 = psum - p_hi.astype(F32)
    p_mid = r1.astype(BF16)
    p_lo = (r1 - p_mid.astype(F32)).astype(BF16)
    imp_w = imp_ref[...]
    imp = _dot(p_hi, imp_w) + _dot(p_mid, imp_w) + _dot(p_lo, imp_w)
    sid = _iota((qb, ns), 1)
    cur = tq // SEL_BLOCK
    valid = sid <= cur
    forced = (sid == 0) | (sid == cur) | (sid == cur - 1)
    score = jnp.where(valid, imp + jnp.where(forced, FORCE_SCORE, 0.0), -1.0)
    sid_f = sid.astype(F32)

    def pick_one(_, carry):
        work, chosen = carry
        mx = jnp.max(work, axis=-1, keepdims=True)
        first = jnp.min(jnp.where(work == mx, sid_f, float(ns)), axis=-1, keepdims=True)
        pick = sid_f == first
        chosen = jnp.where(pick & (mx >= 0.0), 1.0, chosen)
        return jnp.where(pick, -2.0, work), chosen

    _, chosen = lax.fori_loop(0, n_sel, pick_one, (score, jnp.zeros((qb, ns), F32)))
    chosen = chosen.astype(BF16)

    tk = SEL_KEY_TILE
    n_tiles = (q0 + qb + tk - 1) // tk

    def sel_tile(kt, carry):
        ms, ls, accs = carry
        k0 = pl.multiple_of(kt * tk, tk)
        blk = kv_ref[0, pl.ds(k0, tk), :]
        ks, vs = blk[:, 0:dh], blk[:, dh:2 * dh]
        expand = (_iota((ns, tk), 0) == (k0 + _iota((ns, tk), 1)) // SEL_BLOCK)
        in_sel = _dot(chosen, jnp.where(expand, 1.0, 0.0).astype(BF16)) > 0.5
        mk = in_sel & ((k0 + _iota((qb, tk), 1)) <= tq)
        ms2, ls2, accs2 = [], [], []
        for h in range(NSA_HEADS):
            s = jnp.where(mk, _dot_nt(qh[h], ks), NEG_INF)
            m_new = jnp.maximum(ms[h], jnp.max(s, axis=-1, keepdims=True))
            alpha = jnp.exp(ms[h] - m_new)
            p = jnp.where(mk, jnp.exp(s - m_new), 0.0)
            ms2.append(m_new)
            ls2.append(alpha * ls[h] + jnp.sum(p, axis=-1, keepdims=True))
            accs2.append(alpha * accs[h] + _dot(p.astype(BF16), vs))
        return tuple(ms2), tuple(ls2), tuple(accs2)

    init = (tuple(jnp.full((qb, 1), NEG_INF, F32) for _ in range(NSA_HEADS)),
            tuple(jnp.zeros((qb, 1), F32) for _ in range(NSA_HEADS)),
            tuple(jnp.zeros((qb, dh), F32) for _ in range(NSA_HEADS)))
    _, ls, accs = lax.fori_loop(0, n_tiles, sel_tile, init)

    wk = WINDOW + qb
    start = pl.multiple_of(jnp.maximum(q0 - WINDOW, 0), qb)
    blk = kv_ref[0, pl.ds(start, wk), :]
    kw, vw = blk[:, 2 * dh:3 * dh], blk[:, 3 * dh:4 * dh]
    kpos = start + _iota((qb, wk), 1)
    m_w = (kpos <= tq) & (kpos > tq - WINDOW)
    gates = _sigmoid(gl_ref[0])
    for h in range(NSA_HEADS):
        s = jnp.where(m_w, _dot_nt(qh[h], kw), NEG_INF)
        p = jnp.where(m_w, jnp.exp(s - jnp.max(s, axis=-1, keepdims=True)), 0.0)
        p = p / jnp.sum(p, axis=-1, keepdims=True)
        o_w = _dot(p.astype(BF16), vw)
        o_s = accs[h] / ls[h]
        g = gates[:, 3 * h:3 * h + 3]
        o = g[:, 0:1] * o_c[h] + g[:, 1:2] * o_s + g[:, 2:3] * o_w
        o_ref[0, :, h * dh:(h + 1) * dh] = o.astype(o_ref.dtype)


def _nsa_attention(q3, gl3, kcv3, kv3, imp_w):
    b, t, qw = q3.shape
    ncp = kcv3.shape[1]
    ns = t // SEL_BLOCK
    n_sel = min(SEL_TOPK, ns)
    assert t >= WINDOW + Q_BLOCK and t % SEL_KEY_TILE == 0
    return pl.pallas_call(
        functools.partial(_nsa_kernel, n_sel=n_sel),
        grid=(b, t // Q_BLOCK),
        in_specs=[pl.BlockSpec((1, Q_BLOCK, qw), lambda bi, i: (bi, i, 0)),
                  pl.BlockSpec((1, Q_BLOCK, LANES), lambda bi, i: (bi, i, 0)),
                  pl.BlockSpec((1, ncp, 2 * NSA_DH), lambda bi, i: (bi, 0, 0)),
                  pl.BlockSpec((1, t, 4 * NSA_DH), lambda bi, i: (bi, 0, 0)),
                  pl.BlockSpec((ncp, ns), lambda bi, i: (0, 0))],
        out_specs=pl.BlockSpec((1, Q_BLOCK, qw), lambda bi, i: (bi, i, 0)),
        out_shape=jax.ShapeDtypeStruct((b, t, qw), BF16),
        compiler_params=_cparams("parallel", "parallel"),
        name="nsa_attention",
    )(q3, gl3, kcv3, kv3, imp_w)


def _out_proj_kernel(p_ref, n_ref, s_ref, c_ref, x_ref, w_ref, g_ref, b_ref, o_ref, *, alpha):
    y = jnp.concatenate([p_ref[...], n_ref[...], s_ref[...], c_ref[...]], axis=-1)
    mix = _dot(y, w_ref[...])
    o_ref[...] = _layer_norm(alpha * x_ref[...] + mix, g_ref[...], b_ref[...])


def _out_proj_ln(parts, x2, w_out, g, b, alpha):
    n, d = x2.shape
    tm = ROW_TILE
    pw = parts[0].shape[1]
    return pl.pallas_call(
        functools.partial(_out_proj_kernel, alpha=alpha),
        grid=(n // tm,),
        in_specs=[pl.BlockSpec((tm, pw), lambda i: (i, 0)) for _ in parts]
        + [pl.BlockSpec((tm, d), lambda i: (i, 0)),
           pl.BlockSpec(w_out.shape, lambda i: (0, 0)),
           pl.BlockSpec((1, d), lambda i: (0, 0)), pl.BlockSpec((1, d), lambda i: (0, 0))],
        out_specs=pl.BlockSpec((tm, d), lambda i: (i, 0)),
        out_shape=jax.ShapeDtypeStruct((n, d), F32),
        compiler_params=_cparams("parallel"),
        name="out_proj_ln",
    )(*parts, x2, w_out, g, b)


def _matmul_kernel(a_ref, w_ref, o_ref):
    o_ref[...] = _dot(a_ref[...].astype(BF16), w_ref[...]).astype(o_ref.dtype)


def _matmul(a, w, out_dtype):
    n, k = a.shape
    m = w.shape[1]
    tm = min(ROW_TILE, n)
    return pl.pallas_call(
        _matmul_kernel,
        grid=(n // tm,),
        in_specs=[pl.BlockSpec((tm, k), lambda i: (i, 0)), pl.BlockSpec((k, m), lambda i: (0, 0))],
        out_specs=pl.BlockSpec((tm, m), lambda i: (i, 0)),
        out_shape=jax.ShapeDtypeStruct((n, m), out_dtype),
        compiler_params=_cparams("parallel"),
        name="mem_kv_proj",
    )(a, w)


def _xattn_kernel(x_ref, kv_ref, wq_ref, wo_ref, g_ref, b_ref, o_ref, *, alpha):
    x = x_ref[0]
    d = x.shape[1]
    dh = d // X_HEADS
    q = (_dot(x.astype(BF16), wq_ref[...]) * (dh ** -0.5)).astype(BF16)
    kv = kv_ref[0]
    outs = []
    for h in range(X_HEADS):
        k = kv[:, h * dh:(h + 1) * dh]
        v = kv[:, d + h * dh:d + (h + 1) * dh]
        s = _dot_nt(q[:, h * dh:(h + 1) * dh], k)
        p = jnp.exp(s - jnp.max(s, axis=-1, keepdims=True))
        p = p / jnp.sum(p, axis=-1, keepdims=True)
        outs.append(_dot(p.astype(BF16), v).astype(BF16))
    att = _dot(jnp.concatenate(outs, axis=-1), wo_ref[...])
    o_ref[0] = _layer_norm(alpha * x + att, g_ref[...], b_ref[...])


def _xattn_ln(x3, kv3, wq, wo, g, b, alpha):
    bsz, t, d = x3.shape
    m = kv3.shape[1]
    tt = ROW_TILE
    full = lambda bi, i: (0, 0)
    return pl.pallas_call(
        functools.partial(_xattn_kernel, alpha=alpha),
        grid=(bsz, t // tt),
        in_specs=[pl.BlockSpec((1, tt, d), lambda bi, i: (bi, i, 0)),
                  pl.BlockSpec((1, m, 2 * d), lambda bi, i: (bi, 0, 0)),
                  pl.BlockSpec((d, d), full), pl.BlockSpec((d, d), full),
                  pl.BlockSpec((1, d), full), pl.BlockSpec((1, d), full)],
        out_specs=pl.BlockSpec((1, tt, d), lambda bi, i: (bi, i, 0)),
        out_shape=jax.ShapeDtypeStruct((bsz, t, d), F32),
        compiler_params=_cparams("parallel", "parallel"),
        name="xattn_ln",
    )(x3, kv3, wq, wo, g, b)


def _ffn_kernel(x_ref, wg_ref, wu_ref, w2_ref, g_ref, b_ref, o_ref, xb_ref, acc_ref, *, alpha):
    f = pl.program_id(1)

    @pl.when(f == 0)
    def _():
        xb_ref[...] = x_ref[...].astype(BF16)
        acc_ref[...] = jnp.zeros_like(acc_ref)

    xb = xb_ref[...]
    gate = _dot(xb, wg_ref[...])
    up = _dot(xb, wu_ref[...])
    act = (gate * _sigmoid(gate) * up).astype(BF16)
    acc_ref[...] += _dot(act, w2_ref[...])

    @pl.when(f == pl.num_programs(1) - 1)
    def _():
        o_ref[...] = _layer_norm(alpha * x_ref[...] + acc_ref[...], g_ref[...], b_ref[...])


def _ffn_ln(x2, w13, w2, g, b, alpha):
    n, d = x2.shape
    ff = w2.shape[0]
    tm, tf = FFN_ROW_TILE, FFN_COL_TILE
    nf = ff // tf
    return pl.pallas_call(
        functools.partial(_ffn_kernel, alpha=alpha),
        grid=(n // tm, nf),
        in_specs=[pl.BlockSpec((tm, d), lambda i, f: (i, 0)),
                  pl.BlockSpec((d, tf), lambda i, f: (0, f)),
                  pl.BlockSpec((d, tf), lambda i, f: (0, f + nf)),
                  pl.BlockSpec((tf, d), lambda i, f: (f, 0)),
                  pl.BlockSpec((1, d), lambda i, f: (0, 0)), pl.BlockSpec((1, d), lambda i, f: (0, 0))],
        out_specs=pl.BlockSpec((tm, d), lambda i, f: (i, 0)),
        out_shape=jax.ShapeDtypeStruct((n, d), F32),
        scratch_shapes=[pltpu.VMEM((tm, d), BF16), pltpu.VMEM((tm, d), F32)],
        compiler_params=_cparams("parallel", "arbitrary"),
        name="ffn_ln",
    )(x2, w13, w13, w2, g, b)


def _router_kernel(x_ref, rw_ref, tri_ref, ii_ref, if_ref, cnt_ref, carry_ref):
    @pl.when(pl.program_id(0) == 0)
    def _():
        carry_ref[...] = jnp.zeros_like(carry_ref)

    logits = _dot(x_ref[...].astype(BF16), rw_ref[...])
    lane = _iota(logits.shape, 1)
    lf = lane.astype(F32)
    low = np.float32(-3e38)
    lg = jnp.where(lane < N_EXPERTS, logits, low)
    m1 = jnp.max(lg, axis=-1, keepdims=True)
    i1 = jnp.min(jnp.where(lg == m1, lf, float(LANES)), axis=-1, keepdims=True)
    lg2 = jnp.where(lf == i1, low, lg)
    m2 = jnp.max(lg2, axis=-1, keepdims=True)
    i2 = jnp.min(jnp.where(lg2 == m2, lf, float(LANES)), axis=-1, keepdims=True)
    e = jnp.exp(m2 - m1)
    g1 = 1.0 / (1.0 + e)
    g2 = e / (1.0 + e)
    hit1 = lf == i1
    hit2 = lf == i2
    onehot = jnp.where(hit1 | hit2, 1.0, 0.0)
    ranks = _dot(tri_ref[...], onehot.astype(BF16)) + carry_ref[...]
    r1 = jnp.sum(jnp.where(hit1, ranks, 0.0), axis=-1, keepdims=True)
    r2 = jnp.sum(jnp.where(hit2, ranks, 0.0), axis=-1, keepdims=True)
    carry_ref[...] += jnp.sum(onehot, axis=0, keepdims=True)
    info = jnp.where(lane == 0, i1, jnp.where(lane == 1, i2, jnp.where(lane == 2, r1, jnp.where(lane == 3, r2, 0.0))))
    ii_ref[...] = info.astype(jnp.int32)
    if_ref[...] = jnp.where(lane == 0, g1, jnp.where(lane == 1, g2, 0.0))
    cnt_ref[...] = carry_ref[...]


def _router(x2, rw_pad, tri):
    n, d = x2.shape
    tm = ROW_TILE
    return pl.pallas_call(
        _router_kernel,
        grid=(n // tm,),
        in_specs=[pl.BlockSpec((tm, d), lambda i: (i, 0)),
                  pl.BlockSpec((d, LANES), lambda i: (0, 0)),
                  pl.BlockSpec((tm, tm), lambda i: (0, 0))],
        out_specs=[pl.BlockSpec((tm, LANES), lambda i: (i, 0)),
                   pl.BlockSpec((tm, LANES), lambda i: (i, 0)),
                   pl.BlockSpec((1, LANES), lambda i: (0, 0))],
        out_shape=[jax.ShapeDtypeStruct((n, LANES), jnp.int32),
                   jax.ShapeDtypeStruct((n, LANES), F32),
                   jax.ShapeDtypeStruct((1, LANES), F32)],
        scratch_shapes=[pltpu.VMEM((1, LANES), F32)],
        compiler_params=_cparams("arbitrary"),
        name="moe_router",
    )(x2, rw_pad, tri)


def _row_copy(src_hbm, row, dst_ref, r, sem):
    return pltpu.make_async_copy(src_hbm.at[pl.ds(row, 1)], dst_ref.at[pl.ds(r, 1)], sem)


def _gather_kernel(tok_ref, nrows_ref, x_hbm, o_ref, sem):
    i = pl.program_id(0)
    rows = o_ref.shape[0]
    base = i * rows

    @pl.when(base < nrows_ref[0])
    def _():
        def issue(r, c):
            _row_copy(x_hbm, tok_ref[base + r], o_ref, r, sem).start()
            return c

        lax.fori_loop(0, rows, issue, 0)

        def drain(r, c):
            _row_copy(x_hbm, 0, o_ref, r, sem).wait()
            return c

        lax.fori_loop(0, rows, drain, 0)

    @pl.when(base >= nrows_ref[0])
    def _():
        o_ref[...] = jnp.zeros_like(o_ref)


def _gather_rows(tok_sorted, n_rows, x2):
    m_pad = tok_sorted.shape[0]
    d = x2.shape[1]
    rows = GATHER_ROWS
    return pl.pallas_call(
        _gather_kernel,
        grid_spec=pltpu.PrefetchScalarGridSpec(
            num_scalar_prefetch=2,
            grid=(m_pad // rows,),
            in_specs=[pl.BlockSpec(memory_space=pl.ANY)],
            out_specs=pl.BlockSpec((rows, d), lambda i, tok, nr: (i, 0)),
            scratch_shapes=[pltpu.SemaphoreType.DMA(())]),
        out_shape=jax.ShapeDtypeStruct((m_pad, d), x2.dtype),
        compiler_params=_cparams("arbitrary"),
        name="moe_gather",
    )(tok_sorted, n_rows, x2)


def _expert_kernel(blk_e_ref, nblk_ref, x_ref, wg_ref, wu_ref, w2_ref, o_ref, xb_ref, acc_ref):
    i = pl.program_id(0)
    f = pl.program_id(1)
    last = pl.num_programs(1) - 1
    live = i < nblk_ref[0]

    @pl.when(live & (f == 0))
    def _():
        xb_ref[...] = x_ref[...].astype(BF16)
        acc_ref[...] = jnp.zeros_like(acc_ref)

    @pl.when(live)
    def _():
        xb = xb_ref[...]
        gate = _dot(xb, wg_ref[0])
        up = _dot(xb, wu_ref[0])
        act = (gate * _sigmoid(gate) * up).astype(BF16)
        acc_ref[...] += _dot(act, w2_ref[0])

    @pl.when(live & (f == last))
    def _():
        o_ref[...] = acc_ref[...]

    @pl.when(jnp.logical_not(live) & (f == last))
    def _():
        o_ref[...] = jnp.zeros_like(o_ref)


def _expert_ffn(blk_e, n_blk_real, xs, w13, w2):
    m_pad, d = xs.shape
    ff = w2.shape[1]
    bm, tf = MOE_ROW_TILE, MOE_COL_TILE
    nf = ff // tf
    n_blk = m_pad // bm

    def row_map(i, f, be, nb):
        return (jnp.minimum(i, nb[0] - 1), 0)

    def fcol(i, f, nb):
        return jnp.where(i < nb[0], f, nf - 1)

    def e_of(i, be, nb):
        return be[jnp.minimum(i, nb[0] - 1)]

    return pl.pallas_call(
        _expert_kernel,
        grid_spec=pltpu.PrefetchScalarGridSpec(
            num_scalar_prefetch=2,
            grid=(n_blk, nf),
            in_specs=[pl.BlockSpec((bm, d), row_map),
                      pl.BlockSpec((1, d, tf), lambda i, f, be, nb: (e_of(i, be, nb), 0, fcol(i, f, nb))),
                      pl.BlockSpec((1, d, tf), lambda i, f, be, nb: (e_of(i, be, nb), 0, fcol(i, f, nb) + nf)),
                      pl.BlockSpec((1, tf, d), lambda i, f, be, nb: (e_of(i, be, nb), fcol(i, f, nb), 0))],
            out_specs=pl.BlockSpec((bm, d), lambda i, f, be, nb: (i, 0)),
            scratch_shapes=[pltpu.VMEM((bm, d), BF16), pltpu.VMEM((bm, d), F32)]),
        out_shape=jax.ShapeDtypeStruct((m_pad, d), F32),
        compiler_params=_cparams("arbitrary", "arbitrary"),
        name="moe_experts",
    )(blk_e, n_blk_real, xs, w13, w13, w2)


def _combine_kernel(d1_ref, d2_ref, y_hbm, x_ref, gate_ref, g_ref, b_ref, o_ref, y1_ref, y2_ref, sem1, sem2, *, alpha):
    i = pl.program_id(0)
    rows = x_ref.shape[0]
    base = i * rows

    def issue(r, c):
        _row_copy(y_hbm, d1_ref[base + r], y1_ref, r, sem1).start()
        _row_copy(y_hbm, d2_ref[base + r], y2_ref, r, sem2).start()
        return c

    lax.fori_loop(0, rows, issue, 0)

    def drain(r, c):
        _row_copy(y_hbm, 0, y1_ref, r, sem1).wait()
        _row_copy(y_hbm, 0, y2_ref, r, sem2).wait()
        return c

    lax.fori_loop(0, rows, drain, 0)
    gates = gate_ref[...]
    moe = gates[:, 0:1] * y1_ref[...] + gates[:, 1:2] * y2_ref[...]
    o_ref[...] = _layer_norm(alpha * x_ref[...] + moe, g_ref[...], b_ref[...])


def _combine_ln(dest1, dest2, ys, x2, gates, g, b, alpha):
    n, d = x2.shape
    rows = GATHER_ROWS
    return pl.pallas_call(
        functools.partial(_combine_kernel, alpha=alpha),
        grid_spec=pltpu.PrefetchScalarGridSpec(
            num_scalar_prefetch=2,
            grid=(n // rows,),
            in_specs=[pl.BlockSpec(memory_space=pl.ANY),
                      pl.BlockSpec((rows, d), lambda i, a, c: (i, 0)),
                      pl.BlockSpec((rows, LANES), lambda i, a, c: (i, 0)),
                      pl.BlockSpec((1, d), lambda i, a, c: (0, 0)),
                      pl.BlockSpec((1, d), lambda i, a, c: (0, 0))],
            out_specs=pl.BlockSpec((rows, d), lambda i, a, c: (i, 0)),
            scratch_shapes=[pltpu.VMEM((rows, d), F32), pltpu.VMEM((rows, d), F32),
                            pltpu.SemaphoreType.DMA(()), pltpu.SemaphoreType.DMA(())]),
        out_shape=jax.ShapeDtypeStruct((n, d), F32),
        compiler_params=_cparams("arbitrary"),
        name="moe_combine_ln",
    )(dest1, dest2, ys, x2, gates, g, b)


def _moe_ln(x2, router_w, w13, w2, g, b, alpha):
    n, d = x2.shape
    bm = MOE_ROW_TILE
    rw_pad = jnp.zeros((d, LANES), F32).at[:, :N_EXPERTS].set(router_w).astype(BF16)
    tri = jnp.asarray(np.tril(np.ones((ROW_TILE, ROW_TILE), np.float32), -1), BF16)
    info_i, info_f, cnt = _router(x2, rw_pad, tri)
    counts = cnt[0, :N_EXPERTS].astype(jnp.int32)
    pcounts = (counts + bm - 1) // bm * bm
    pend = jnp.cumsum(pcounts)
    pstart = pend - pcounts
    dest1 = pstart[info_i[:, 0]] + info_i[:, 2]
    dest2 = pstart[info_i[:, 1]] + info_i[:, 3]
    m_pad = 2 * n + N_EXPERTS * bm
    tok = jnp.arange(n, dtype=jnp.int32)
    tok_sorted = jnp.zeros((m_pad,), jnp.int32).at[dest1].set(tok).at[dest2].set(tok)
    n_blk = m_pad // bm
    blk_e = jnp.minimum(jnp.searchsorted(pend, jnp.arange(n_blk, dtype=jnp.int32) * bm, side="right"),
                        N_EXPERTS - 1).astype(jnp.int32)
    n_rows = pend[-1:].astype(jnp.int32)
    xs = _gather_rows(tok_sorted, n_rows, x2)
    ys = _expert_ffn(blk_e, n_rows // bm, xs, w13, w2)
    return _combine_ln(dest1.astype(jnp.int32), dest2.astype(jnp.int32), ys, x2, info_f, g, b, alpha)


def _block_diag(blocks):
    g, r, c = blocks.shape
    out = jnp.zeros((g * r, g * c), blocks.dtype)
    for k in range(g):
        out = out.at[k * r:(k + 1) * r, k * c:(k + 1) * c].set(blocks[k])
    return out


def _pack_w_in(w_in, d):
    pw = nw = sw = cw = d // 4
    dh = NSA_DH
    o = np.cumsum([0, pw, nw, 6 * dh, 3 * NSA_HEADS, 2 * sw, 2 * cw])
    a, q, kv, gl, uv, ag = (w_in[:, o[k]:o[k + 1]] for k in range(6))
    gl_pad = jnp.zeros((d, LANES), w_in.dtype).at[:, :gl.shape[1]].set(gl)
    packed = jnp.concatenate([ag, uv, a, kv[:, :2 * dh], gl_pad, q * (dh ** -0.5), kv[:, 2 * dh:]], axis=1)
    return packed.astype(BF16)


def _pack_compress(pe_k, w1k, pe_v, w1v, w2k, w2v):
    dh = NSA_DH
    half = CMP_STRIDE
    hid = w1k.shape[1]

    def halves(pe, w1):
        w = w1.reshape(CMP_BLOCK, dh, hid)
        return (pe[:half], pe[half:]), (w[:half], w[half:])

    (pka, pkb), (wka, wkb) = halves(pe_k, w1k)
    (pva, pvb), (wva, wvb) = halves(pe_v, w1v)
    zeros = jnp.zeros((half, dh, hid), w1k.dtype)

    def interleave_w(wk, wv):
        top = jnp.concatenate([wk, zeros], axis=2)
        bot = jnp.concatenate([zeros, wv], axis=2)
        return jnp.concatenate([top, bot], axis=1).reshape(half * 2 * dh, 2 * hid).astype(BF16)

    def interleave_pe(pk, pv):
        return jnp.concatenate([pk, pv], axis=1).reshape(1, half * 2 * dh)

    w2_bd = _block_diag(jnp.stack([w2k, w2v])).astype(BF16)
    return (interleave_pe(pka, pva), interleave_pe(pkb, pvb), interleave_w(wka, wva), interleave_w(wkb, wvb), w2_bd)


def _importance_matrix(t):
    n_chunk = t // CMP_STRIDE
    per = CMP_BLOCK // CMP_STRIDE
    nc = n_chunk - per + 1
    ns = t // SEL_BLOCK
    ratio = SEL_BLOCK // CMP_STRIDE
    lead = per - 1
    m = np.zeros((n_chunk, ns), np.float32)
    for s in range(ns):
        for j in list(range(ratio * s, ratio * s + ratio)) + [ratio + p + ratio * s for p in range(lead)]:
            n = j - lead
            if 0 <= n < nc:
                m[n, s] += 1.0
    return jnp.asarray(m, BF16)


def kernel(x, mem, w_in, pool_w, pool_scale, cmp_pe_k, cmp_k_w1, cmp_k_w2, cmp_pe_v, cmp_v_w1, cmp_v_w2, sgu_ln_g, sgu_ln_b, sgu_w, sgu_b, conv_w, conv_b, conv_ln_g, conv_ln_b, conv_pw, w_out, ln1_g, ln1_b, xq_w, xkv_w, xo_w, ln2_g, ln2_b, ffn_w13, ffn_w2, router_w, exp_w13, exp_w2, ln3_g, ln3_b):
    b, t, d = x.shape
    depth = w_in.shape[0]
    alpha = float((2 * depth) ** 0.25)
    n = b * t
    m = mem.shape[1]
    sw = d // 4
    imp_w = _importance_matrix(t)
    row = lambda v: v.reshape(1, -1)
    x2 = x.reshape(n, d)
    mem2 = mem.reshape(b * m, d)
    for l in range(depth):
        ag, uv, pa, kvc, gl, q, kv = _in_proj(x2, _pack_w_in(w_in[l], d))
        pool_o = _pool_mixer(pa.reshape(b, t, -1), _block_diag(pool_w[l]).astype(BF16), row(pool_scale[l]))
        cmp_w = _pack_compress(cmp_pe_k[l], cmp_k_w1[l], cmp_pe_v[l], cmp_v_w1[l], cmp_k_w2[l], cmp_v_w2[l])
        kcv = _compress(kvc.reshape(b, t // CMP_STRIDE, CMP_STRIDE * 2 * NSA_DH), *cmp_w)
        nsa_o = _nsa_attention(q.reshape(b, t, -1), gl.reshape(b, t, -1), kcv, kv.reshape(b, t, -1), imp_w)
        bias_full = jnp.repeat(sgu_b[l].T, sw // SGU_GROUPS, axis=1)
        sgu_o = _sgu_mixer(uv, row(sgu_ln_g[l]), row(sgu_ln_b[l]), sgu_w[l], bias_full)
        conv_o = _conv_mixer(ag.reshape(b, t, -1), conv_w[l], row(conv_b[l]), row(conv_ln_g[l]),
                             row(conv_ln_b[l]), conv_pw[l].astype(BF16))
        parts = [pool_o.reshape(n, -1), nsa_o.reshape(n, -1), sgu_o, conv_o.reshape(n, -1)]
        x2 = _out_proj_ln(parts, x2, w_out[l].astype(BF16), row(ln1_g[l]), row(ln1_b[l]), alpha)
        kvm = _matmul(mem2, xkv_w[l].astype(BF16), BF16)
        x2 = _xattn_ln(x2.reshape(b, t, d), kvm.reshape(b, m, 2 * d), xq_w[l].astype(BF16),
                       xo_w[l].astype(BF16), row(ln2_g[l]), row(ln2_b[l]), alpha).reshape(n, d)
        if l % 2 == 0:
            x2 = _ffn_ln(x2, ffn_w13[l // 2].astype(BF16), ffn_w2[l // 2].astype(BF16),
                         row(ln3_g[l]), row(ln3_b[l]), alpha)
        else:
            x2 = _moe_ln(x2, router_w[l // 2], exp_w13[l // 2].astype(BF16), exp_w2[l // 2].astype(BF16),
                         row(ln3_g[l]), row(ln3_b[l]), alpha)
    return x2.reshape(b, t, d)
```

```python
import functools

import numpy as np
import jax
import jax.numpy as jnp
from jax import lax
from jax.experimental import pallas as pl
from jax.experimental.pallas import tpu as pltpu

F32 = jnp.float32
BF16 = jnp.bfloat16

POOL_GROUPS = 4
POOL_WINDOWS = (2, 4, 8, 16)
POOL_HALO = 16
NSA_HEADS = 4
NSA_DH = 64
CMP_BLOCK = 32
CMP_STRIDE = 16
SEL_BLOCK = 64
SEL_TOPK = 16
WINDOW = 512
Q_BLOCK = 128
FORCE_SCORE = 1e4
SGU_GROUPS = 4
SGU_CHUNK = 128
CONV_WIDTH = 31
CONV_HALO = 32
X_HEADS = 4
N_EXPERTS = 8
LN_EPS = 1e-5
NEG_INF = -1e30

LANES = 128
VMEM_LIMIT = 56 * 1024 * 1024
ROW_TILE = 512
SEL_KEY_TILE = 512
FFN_ROW_TILE = 1024
FFN_COL_TILE = 256
MOE_ROW_TILE = 1024
MOE_COL_TILE = 512
GATHER_ROWS = 256


def _cparams(*sem):
    return pltpu.CompilerParams(dimension_semantics=sem, vmem_limit_bytes=VMEM_LIMIT)


def _layer_norm(x, g, b):
    mu = jnp.mean(x, axis=-1, keepdims=True)
    xc = x - mu
    var = jnp.mean(xc * xc, axis=-1, keepdims=True)
    return xc * lax.rsqrt(var + LN_EPS) * g + b


def _sigmoid(x):
    return 1.0 / (1.0 + jnp.exp(-x))


def _gelu_tanh(x):
    c = np.float32(np.sqrt(2.0 / np.pi))
    return 0.5 * x * (1.0 + jnp.tanh(c * (x + np.float32(0.044715) * (x * x * x))))


def _dot(a, b):
    return jnp.dot(a, b, preferred_element_type=F32)


def _dot_nt(a, b):
    return lax.dot_general(a, b, (((1,), (1,)), ((), ())), preferred_element_type=F32)


def _iota(shape, dim):
    return lax.broadcasted_iota(jnp.int32, shape, dim)


_IN_COLS = (("ag", 512, F32), ("uv", 512, F32), ("pool", 256, F32), ("kvc", 128, F32),
            ("gl", 128, F32), ("q", 256, BF16), ("kv", 256, BF16))


def _in_proj_kernel(x_ref, w_ref, *out_refs):
    xb = x_ref[...].astype(BF16)
    off = 0
    for (_, width, dt), o_ref in zip(_IN_COLS, out_refs):
        o_ref[...] = _dot(xb, w_ref[:, off:off + width]).astype(dt)
        off += width


def _in_proj(x2, w_packed):
    n, d = x2.shape
    tm = ROW_TILE
    tot = sum(w for _, w, _ in _IN_COLS)
    return pl.pallas_call(
        _in_proj_kernel,
        grid=(n // tm,),
        in_specs=[pl.BlockSpec((tm, d), lambda i: (i, 0)),
                  pl.BlockSpec((d, tot), lambda i: (0, 0))],
        out_specs=[pl.BlockSpec((tm, w), lambda i: (i, 0)) for _, w, _ in _IN_COLS],
        out_shape=[jax.ShapeDtypeStruct((n, w), dt) for _, w, dt in _IN_COLS],
        compiler_params=_cparams("parallel"),
        name="in_proj",
    )(x2, w_packed)


def _pool_kernel(a_ref, halo_ref, w_ref, scale_ref, o_ref):
    i = pl.program_id(1)
    tt = a_ref.shape[1]
    a = a_ref[0]
    halo = jnp.where(i > 0, halo_ref[0], 0.0)
    ext = jnp.concatenate([halo, a], axis=0)
    s2 = ext + pltpu.roll(ext, 1, 0)
    s4 = s2 + pltpu.roll(s2, 2, 0)
    s8 = s4 + pltpu.roll(s4, 4, 0)
    s16 = s8 + pltpu.roll(s8, 8, 0)
    grp = _iota(a.shape, 1) // (a.shape[1] // POOL_GROUPS)
    pos = i * tt + _iota(a.shape, 0)
    h = POOL_HALO
    s = jnp.where(grp == 0, s2[h:], jnp.where(grp == 1, s4[h:], jnp.where(grp == 2, s8[h:], s16[h:])))
    win = jnp.where(grp == 0, 2, jnp.where(grp == 1, 4, jnp.where(grp == 2, 8, 16)))
    cnt = jnp.minimum(pos + 1, win).astype(F32)
    d = s / cnt - a
    y = _dot(d.astype(BF16), w_ref[...]) * scale_ref[...]
    o_ref[0] = y.astype(o_ref.dtype)


def _pool_mixer(a3, w_bd, scale):
    b, t, c = a3.shape
    tt = ROW_TILE
    hb = tt // POOL_HALO
    return pl.pallas_call(
        _pool_kernel,
        grid=(b, t // tt),
        in_specs=[pl.BlockSpec((1, tt, c), lambda bi, i: (bi, i, 0)),
                  pl.BlockSpec((1, POOL_HALO, c), lambda bi, i: (bi, jnp.maximum(i * hb - 1, 0), 0)),
                  pl.BlockSpec((c, c), lambda bi, i: (0, 0)),
                  pl.BlockSpec((1, c), lambda bi, i: (0, 0))],
        out_specs=pl.BlockSpec((1, tt, c), lambda bi, i: (bi, i, 0)),
        out_shape=jax.ShapeDtypeStruct((b, t, c), BF16),
        compiler_params=_cparams("parallel", "parallel"),
        name="pool_mixer",
    )(a3, a3, w_bd, scale)


def _conv_kernel(ag_ref, halo_ref, wdw_ref, bdw_ref, g_ref, b_ref, wpw_ref, o_ref):
    i = pl.program_id(1)
    tt = ag_ref.shape[1]
    cw = ag_ref.shape[2] // 2
    ag = jnp.concatenate([halo_ref[0], ag_ref[0]], axis=0)
    h = ag[:, :cw] * _sigmoid(ag[:, cw:])
    row = _iota(h.shape, 0)
    h = jnp.where((row >= CONV_HALO) | (i > 0), h, 0.0)
    n_ext = tt + CONV_HALO
    lead = CONV_HALO - (CONV_WIDTH - 1)
    acc = jnp.zeros((tt, cw), F32)
    for r in range(8):
        hr = h if r == 0 else pltpu.roll(h, n_ext - r, 0)
        for k in range(CONV_WIDTH):
            if (lead + k) % 8 == r:
                off = lead + k - r
                acc = acc + hr[off:off + tt] * wdw_ref[k:k + 1, :]
    y = acc + bdw_ref[...]
    y = _layer_norm(y, g_ref[...], b_ref[...])
    y = y * _sigmoid(y)
    o_ref[0] = _dot(y.astype(BF16), wpw_ref[...]).astype(o_ref.dtype)


def _conv_mixer(ag3, w_dw, b_dw, ln_g, ln_b, w_pw):
    b, t, c2 = ag3.shape
    cw = c2 // 2
    tt = ROW_TILE
    hb = tt // CONV_HALO
    full = lambda bi, i: (0, 0)
    return pl.pallas_call(
        _conv_kernel,
        grid=(b, t // tt),
        in_specs=[pl.BlockSpec((1, tt, c2), lambda bi, i: (bi, i, 0)),
                  pl.BlockSpec((1, CONV_HALO, c2), lambda bi, i: (bi, jnp.maximum(i * hb - 1, 0), 0)),
                  pl.BlockSpec((CONV_WIDTH, cw), full),
                  pl.BlockSpec((1, cw), full), pl.BlockSpec((1, cw), full), pl.BlockSpec((1, cw), full),
                  pl.BlockSpec((cw, cw), full)],
        out_specs=pl.BlockSpec((1, tt, cw), lambda bi, i: (bi, i, 0)),
        out_shape=jax.ShapeDtypeStruct((b, t, cw), BF16),
        compiler_params=_cparams("parallel", "parallel"),
        name="conv_mixer",
    )(ag3, ag3, w_dw, b_dw, ln_g, ln_b, w_pw)


def _sgu_kernel(uv_ref, g_ref, b_ref, ws_ref, bs_ref, o_ref):
    tt = uv_ref.shape[0]
    sw = uv_ref.shape[1] // 2
    gw = sw // SGU_GROUPS
    act = _gelu_tanh(uv_ref[...])
    u = act[:, :sw]
    v = _layer_norm(act[:, sw:], g_ref[...], b_ref[...]).astype(BF16)
    causal = _iota((SGU_CHUNK, SGU_CHUNK), 1) <= _iota((SGU_CHUNK, SGU_CHUNK), 0)
    ws = [jnp.where(causal, ws_ref[g], 0.0).astype(BF16) for g in range(SGU_GROUPS)]
    lane_grp = _iota((SGU_CHUNK, sw), 1) // gw
    for ch in range(tt // SGU_CHUNK):
        rows = slice(ch * SGU_CHUNK, (ch + 1) * SGU_CHUNK)
        vc = v[rows]
        mix = jnp.zeros((SGU_CHUNK, sw), F32)
        for g in range(SGU_GROUPS):
            mix = jnp.where(lane_grp == g, _dot(ws[g], vc), mix)
        o_ref[rows, :] = (u[rows] * (mix + bs_ref[...])).astype(o_ref.dtype)


def _sgu_mixer(uv2, ln_g, ln_b, w_s, bias_full):
    n, c2 = uv2.shape
    sw = c2 // 2
    tm = ROW_TILE
    return pl.pallas_call(
        _sgu_kernel,
        grid=(n // tm,),
        in_specs=[pl.BlockSpec((tm, c2), lambda i: (i, 0)),
                  pl.BlockSpec((1, sw), lambda i: (0, 0)), pl.BlockSpec((1, sw), lambda i: (0, 0)),
                  pl.BlockSpec((SGU_GROUPS, SGU_CHUNK, SGU_CHUNK), lambda i: (0, 0, 0)),
                  pl.BlockSpec((SGU_CHUNK, sw), lambda i: (0, 0))],
        out_specs=pl.BlockSpec((tm, sw), lambda i: (i, 0)),
        out_shape=jax.ShapeDtypeStruct((n, sw), BF16),
        compiler_params=_cparams("parallel"),
        name="sgu_mixer",
    )(uv2, ln_g, ln_b, w_s, bias_full)


def _compress_kernel(rc_ref, pea_ref, peb_ref, wa_ref, wb_ref, w2_ref, o_ref):
    x = rc_ref[0]
    nch = x.shape[0]
    first = _dot((x + pea_ref[...]).astype(BF16), wa_ref[...])
    second = _dot((x + peb_ref[...]).astype(BF16), wb_ref[...])
    hid = _gelu_tanh(first + pltpu.roll(second, nch - 1, 0))
    o_ref[0] = _dot(hid.astype(BF16), w2_ref[...]).astype(o_ref.dtype)


def _compress(rc3, pe_a, pe_b, w_a, w_b, w2_bd):
    b, nch, width = rc3.shape
    hid2 = w_a.shape[1]
    full = lambda bi: (0, 0)
    return pl.pallas_call(
        _compress_kernel,
        grid=(b,),
        in_specs=[pl.BlockSpec((1, nch, width), lambda bi: (bi, 0, 0)),
                  pl.BlockSpec((1, width), full), pl.BlockSpec((1, width), full),
                  pl.BlockSpec((width, hid2), full), pl.BlockSpec((width, hid2), full),
                  pl.BlockSpec((hid2, 2 * NSA_DH), full)],
        out_specs=pl.BlockSpec((1, nch, 2 * NSA_DH), lambda bi: (bi, 0, 0)),
        out_shape=jax.ShapeDtypeStruct((b, nch, 2 * NSA_DH), BF16),
        compiler_params=_cparams("parallel"),
        name="nsa_compress",
    )(rc3, pe_a, pe_b, w_a, w_b, w2_bd)


def _nsa_kernel(q_ref, gl_ref, kc_ref, kv_ref, imp_ref, exp_ref, o_ref, *, n_sel):
    i = pl.program_id(1)
    qb = q_ref.shape[1]
    ncp = kc_ref.shape[1]
    ns = imp_ref.shape[1]
    tk = exp_ref.shape[2]
    dh = NSA_DH
    nh = NSA_HEADS
    hq = nh * qb
    q0 = i * qb
    q = q_ref[0]
    qs = jnp.concatenate([q[:, h * dh:(h + 1) * dh] for h in range(nh)], axis=0)
    tq = q0 + _iota((qb, 1), 0)

    def scores(keys, bias):
        return _dot_nt(qs, keys).reshape(nh, qb, keys.shape[0]) + bias[None]

    kcv = kc_ref[0]
    kc, vc = kcv[:, :dh], kcv[:, dh:]
    m_c = (_iota((qb, ncp), 1) * CMP_STRIDE + (CMP_BLOCK - 1)) <= tq
    s = scores(kc, jnp.where(m_c, 0.0, NEG_INF))
    p = jnp.exp(s - jnp.max(s, axis=-1, keepdims=True))
    l = jnp.sum(p, axis=-1, keepdims=True)
    p = p * jnp.where(tq >= CMP_BLOCK - 1, 1.0 / l, 0.0)
    psum = jnp.sum(p, axis=0)
    o_c = _dot(p.reshape(hq, ncp).astype(BF16), vc)

    p_hi = psum.astype(BF16)
    r1 = psum - p_hi.astype(F32)
    p_mid = r1.astype(BF16)
    p_lo = (r1 - p_mid.astype(F32)).astype(BF16)
    imp_w = imp_ref[...]
    imp = _dot(p_hi, imp_w) + _dot(p_mid, imp_w) + _dot(p_lo, imp_w)
    sid = _iota((qb, ns), 1)
    cur = tq // SEL_BLOCK
    forced = (sid == 0) | (sid == cur) | (sid == cur - 1)
    score = jnp.where(sid <= cur, imp + jnp.where(forced, FORCE_SCORE, 0.0), -1.0)

    work = score.T
    valid_t = work >= 0.0
    sid_t = _iota((ns, qb), 0).astype(F32)
    chosen_t = jnp.zeros((ns, qb), F32)
    for _ in range(n_sel):
        mx = jnp.max(work, axis=0, keepdims=True)
        first = jnp.min(jnp.where(work == mx, sid_t, float(ns)), axis=0, keepdims=True)
        pick = sid_t == first
        chosen_t = jnp.where(pick, 1.0, chosen_t)
        work = jnp.where(pick, -2.0, work)
    chosen = jnp.where(valid_t, chosen_t, 0.0).T.astype(BF16)

    wk = WINDOW + qb
    start = pl.multiple_of(jnp.maximum(q0 - WINDOW, 0), qb)
    blk = kv_ref[0, pl.ds(start, wk), :]
    kw, vw = blk[:, 2 * dh:3 * dh], blk[:, 3 * dh:4 * dh]
    kpos = start + _iota((qb, wk), 1)
    s = scores(kw, jnp.where((kpos <= tq) & (kpos > tq - WINDOW), 0.0, NEG_INF))
    p = jnp.exp(s - jnp.max(s, axis=-1, keepdims=True))
    l_w = jnp.sum(p, axis=-1, keepdims=True)
    o_w = _dot(p.reshape(hq, wk).astype(BF16), vw) * (1.0 / l_w).reshape(hq, 1)

    n_tiles = (q0 + qb + tk - 1) // tk

    def sel_tile(kt, carry):
        m_run, l_run, acc = carry
        k0 = pl.multiple_of(kt * tk, tk)
        blk = kv_ref[0, pl.ds(k0, tk), :]
        ks, vs = blk[:, 0:dh], blk[:, dh:2 * dh]
        in_sel = _dot(chosen, exp_ref[kt]) > 0.5
        s = scores(ks, jnp.where(in_sel & ((k0 + _iota((qb, tk), 1)) <= tq), 0.0, NEG_INF))
        m_new = jnp.maximum(m_run, jnp.max(s, axis=-1, keepdims=True))
        alpha = jnp.exp(m_run - m_new)
        p = jnp.exp(s - m_new)
        l_new = alpha * l_run + jnp.sum(p, axis=-1, keepdims=True)
        acc = alpha.reshape(hq, 1) * acc + _dot(p.reshape(hq, tk).astype(BF16), vs)
        return m_new, l_new, acc

    init = (jnp.full((nh, qb, 1), NEG_INF, F32), jnp.zeros((nh, qb, 1), F32), jnp.zeros((hq, dh), F32))
    _, l_s, acc = lax.fori_loop(0, n_tiles, sel_tile, init)
    o_s = acc * (1.0 / l_s).reshape(hq, 1)

    gates = _sigmoid(gl_ref[0])
    gs = jnp.concatenate([gates[:, 3 * h:3 * h + 3] for h in range(nh)], axis=0)
    o = gs[:, 0:1] * o_c + gs[:, 1:2] * o_s + gs[:, 2:3] * o_w
    for h in range(nh):
        o_ref[0, :, h * dh:(h + 1) * dh] = o[h * qb:(h + 1) * qb].astype(o_ref.dtype)


def _nsa_attention(q3, gl3, kcv3, kv3, imp_w, expand):
    b, t, qw = q3.shape
    ncp = kcv3.shape[1]
    ns = t // SEL_BLOCK
    n_sel = min(SEL_TOPK, ns)
    assert t >= WINDOW + Q_BLOCK and t % SEL_KEY_TILE == 0
    return pl.pallas_call(
        functools.partial(_nsa_kernel, n_sel=n_sel),
        grid=(b, t // Q_BLOCK),
        in_specs=[pl.BlockSpec((1, Q_BLOCK, qw), lambda bi, i: (bi, i, 0)),
                  pl.BlockSpec((1, Q_BLOCK, LANES), lambda bi, i: (bi, i, 0)),
                  pl.BlockSpec((1, ncp, 2 * NSA_DH), lambda bi, i: (bi, 0, 0)),
                  pl.BlockSpec((1, t, 4 * NSA_DH), lambda bi, i: (bi, 0, 0)),
                  pl.BlockSpec((ncp, ns), lambda bi, i: (0, 0)),
                  pl.BlockSpec(expand.shape, lambda bi, i: (0, 0, 0))],
        out_specs=pl.BlockSpec((1, Q_BLOCK, qw), lambda bi, i: (bi, i, 0)),
        out_shape=jax.ShapeDtypeStruct((b, t, qw), BF16),
        compiler_params=_cparams("parallel", "parallel"),
        name="nsa_attention",
    )(q3, gl3, kcv3, kv3, imp_w, expand)


def _out_proj_kernel(p_ref, n_ref, s_ref, c_ref, x_ref, w_ref, g_ref, b_ref, o_ref, *, alpha):
    y = jnp.concatenate([p_ref[...], n_ref[...], s_ref[...], c_ref[...]], axis=-1)
    mix = _dot(y, w_ref[...])
    o_ref[...] = _layer_norm(alpha * x_ref[...] + mix, g_ref[...], b_ref[...])


def _out_proj_ln(parts, x2, w_out, g, b, alpha):
    n, d = x2.shape
    tm = ROW_TILE
    pw = parts[0].shape[1]
    return pl.pallas_call(
        functools.partial(_out_proj_kernel, alpha=alpha),
        grid=(n // tm,),
        in_specs=[pl.BlockSpec((tm, pw), lambda i: (i, 0)) for _ in parts]
        + [pl.BlockSpec((tm, d), lambda i: (i, 0)),
           pl.BlockSpec(w_out.shape, lambda i: (0, 0)),
           pl.BlockSpec((1, d), lambda i: (0, 0)), pl.BlockSpec((1, d), lambda i: (0, 0))],
        out_specs=pl.BlockSpec((tm, d), lambda i: (i, 0)),
        out_shape=jax.ShapeDtypeStruct((n, d), F32),
        compiler_params=_cparams("parallel"),
        name="out_proj_ln",
    )(*parts, x2, w_out, g, b)


def _matmul_kernel(a_ref, w_ref, o_ref):
    o_ref[...] = _dot(a_ref[...].astype(BF16), w_ref[...]).astype(o_ref.dtype)


def _matmul(a, w, out_dtype):
    n, k = a.shape
    m = w.shape[1]
    tm = min(ROW_TILE, n)
    return pl.pallas_call(
        _matmul_kernel,
        grid=(n // tm,),
        in_specs=[pl.BlockSpec((tm, k), lambda i: (i, 0)), pl.BlockSpec((k, m), lambda i: (0, 0))],
        out_specs=pl.BlockSpec((tm, m), lambda i: (i, 0)),
        out_shape=jax.ShapeDtypeStruct((n, m), out_dtype),
        compiler_params=_cparams("parallel"),
        name="mem_kv_proj",
    )(a, w)


def _xattn_kernel(x_ref, kv_ref, wq_ref, wo_ref, g_ref, b_ref, o_ref, *, alpha):
    x = x_ref[0]
    d = x.shape[1]
    dh = d // X_HEADS
    q = (_dot(x.astype(BF16), wq_ref[...]) * (dh ** -0.5)).astype(BF16)
    kv = kv_ref[0]
    outs = []
    for h in range(X_HEADS):
        k = kv[:, h * dh:(h + 1) * dh]
        v = kv[:, d + h * dh:d + (h + 1) * dh]
        s = _dot_nt(q[:, h * dh:(h + 1) * dh], k)
        p = jnp.exp(s - jnp.max(s, axis=-1, keepdims=True))
        p = p / jnp.sum(p, axis=-1, keepdims=True)
        outs.append(_dot(p.astype(BF16), v).astype(BF16))
    att = _dot(jnp.concatenate(outs, axis=-1), wo_ref[...])
    o_ref[0] = _layer_norm(alpha * x + att, g_ref[...], b_ref[...])


def _xattn_ln(x3, kv3, wq, wo, g, b, alpha):
    bsz, t, d = x3.shape
    m = kv3.shape[1]
    tt = ROW_TILE
    full = lambda bi, i: (0, 0)
    return pl.pallas_call(
        functools.partial(_xattn_kernel, alpha=alpha),
        grid=(bsz, t // tt),
        in_specs=[pl.BlockSpec((1, tt, d), lambda bi, i: (bi, i, 0)),
                  pl.BlockSpec((1, m, 2 * d), lambda bi, i: (bi, 0, 0)),
                  pl.BlockSpec((d, d), full), pl.BlockSpec((d, d), full),
                  pl.BlockSpec((1, d), full), pl.BlockSpec((1, d), full)],
        out_specs=pl.BlockSpec((1, tt, d), lambda bi, i: (bi, i, 0)),
        out_shape=jax.ShapeDtypeStruct((bsz, t, d), F32),
        compiler_params=_cparams("parallel", "parallel"),
        name="xattn_ln",
    )(x3, kv3, wq, wo, g, b)


def _ffn_kernel(x_ref, wg_ref, wu_ref, w2_ref, g_ref, b_ref, o_ref, xb_ref, acc_ref, *, alpha):
    f = pl.program_id(1)

    @pl.when(f == 0)
    def _():
        xb_ref[...] = x_ref[...].astype(BF16)
        acc_ref[...] = jnp.zeros_like(acc_ref)

    xb = xb_ref[...]
    gate = _dot(xb, wg_ref[...])
    up = _dot(xb, wu_ref[...])
    act = (gate * _sigmoid(gate) * up).astype(BF16)
    acc_ref[...] += _dot(act, w2_ref[...])

    @pl.when(f == pl.num_programs(1) - 1)
    def _():
        o_ref[...] = _layer_norm(alpha * x_ref[...] + acc_ref[...], g_ref[...], b_ref[...])


def _ffn_ln(x2, w13, w2, g, b, alpha):
    n, d = x2.shape
    ff = w2.shape[0]
    tm, tf = FFN_ROW_TILE, FFN_COL_TILE
    nf = ff // tf
    return pl.pallas_call(
        functools.partial(_ffn_kernel, alpha=alpha),
        grid=(n // tm, nf),
        in_specs=[pl.BlockSpec((tm, d), lambda i, f: (i, 0)),
                  pl.BlockSpec((d, tf), lambda i, f: (0, f)),
                  pl.BlockSpec((d, tf), lambda i, f: (0, f + nf)),
                  pl.BlockSpec((tf, d), lambda i, f: (f, 0)),
                  pl.BlockSpec((1, d), lambda i, f: (0, 0)), pl.BlockSpec((1, d), lambda i, f: (0, 0))],
        out_specs=pl.BlockSpec((tm, d), lambda i, f: (i, 0)),
        out_shape=jax.ShapeDtypeStruct((n, d), F32),
        scratch_shapes=[pltpu.VMEM((tm, d), BF16), pltpu.VMEM((tm, d), F32)],
        compiler_params=_cparams("parallel", "arbitrary"),
        name="ffn_ln",
    )(x2, w13, w13, w2, g, b)


def _router_kernel(x_ref, rw_ref, tri_ref, ii_ref, if_ref, cnt_ref, carry_ref):
    @pl.when(pl.program_id(0) == 0)
    def _():
        carry_ref[...] = jnp.zeros_like(carry_ref)

    logits = _dot(x_ref[...].astype(BF16), rw_ref[...])
    lane = _iota(logits.shape, 1)
    lf = lane.astype(F32)
    low = np.float32(-3e38)
    lg = jnp.where(lane < N_EXPERTS, logits, low)
    m1 = jnp.max(lg, axis=-1, keepdims=True)
    i1 = jnp.min(jnp.where(lg == m1, lf, float(LANES)), axis=-1, keepdims=True)
    lg2 = jnp.where(lf == i1, low, lg)
    m2 = jnp.max(lg2, axis=-1, keepdims=True)
    i2 = jnp.min(jnp.where(lg2 == m2, lf, float(LANES)), axis=-1, keepdims=True)
    e = jnp.exp(m2 - m1)
    g1 = 1.0 / (1.0 + e)
    g2 = e / (1.0 + e)
    hit1 = lf == i1
    hit2 = lf == i2
    onehot = jnp.where(hit1 | hit2, 1.0, 0.0)
    ranks = _dot(tri_ref[...], onehot.astype(BF16)) + carry_ref[...]
    r1 = jnp.sum(jnp.where(hit1, ranks, 0.0), axis=-1, keepdims=True)
    r2 = jnp.sum(jnp.where(hit2, ranks, 0.0), axis=-1, keepdims=True)
    carry_ref[...] += jnp.sum(onehot, axis=0, keepdims=True)
    info = jnp.where(lane == 0, i1, jnp.where(lane == 1, i2, jnp.where(lane == 2, r1, jnp.where(lane == 3, r2, 0.0))))
    ii_ref[...] = info.astype(jnp.int32)
    if_ref[...] = jnp.where(lane == 0, g1, jnp.where(lane == 1, g2, 0.0))
    cnt_ref[...] = carry_ref[...]


def _router(x2, rw_pad, tri):
    n, d = x2.shape
    tm = ROW_TILE
    return pl.pallas_call(
        _router_kernel,
        grid=(n // tm,),
        in_specs=[pl.BlockSpec((tm, d), lambda i: (i, 0)),
                  pl.BlockSpec((d, LANES), lambda i: (0, 0)),
                  pl.BlockSpec((tm, tm), lambda i: (0, 0))],
        out_specs=[pl.BlockSpec((tm, LANES), lambda i: (i, 0)),
                   pl.BlockSpec((tm, LANES), lambda i: (i, 0)),
                   pl.BlockSpec((1, LANES), lambda i: (0, 0))],
        out_shape=[jax.ShapeDtypeStruct((n, LANES), jnp.int32),
                   jax.ShapeDtypeStruct((n, LANES), F32),
                   jax.ShapeDtypeStruct((1, LANES), F32)],
        scratch_shapes=[pltpu.VMEM((1, LANES), F32)],
        compiler_params=_cparams("arbitrary"),
        name="moe_router",
    )(x2, rw_pad, tri)


def _row_copy(src_hbm, row, dst_ref, r, sem):
    return pltpu.make_async_copy(src_hbm.at[pl.ds(row, 1)], dst_ref.at[pl.ds(r, 1)], sem)


def _gather_kernel(tok_ref, nrows_ref, x_hbm, o_ref, sem):
    i = pl.program_id(0)
    rows = o_ref.shape[0]
    base = i * rows

    @pl.when(base < nrows_ref[0])
    def _():
        def issue(r, c):
            _row_copy(x_hbm, tok_ref[base + r], o_ref, r, sem).start()
            return c

        lax.fori_loop(0, rows, issue, 0)

        def drain(r, c):
            _row_copy(x_hbm, 0, o_ref, r, sem).wait()
            return c

        lax.fori_loop(0, rows, drain, 0)

    @pl.when(base >= nrows_ref[0])
    def _():
        o_ref[...] = jnp.zeros_like(o_ref)


def _gather_rows(tok_sorted, n_rows, x2):
    m_pad = tok_sorted.shape[0]
    d = x2.shape[1]
    rows = GATHER_ROWS
    return pl.pallas_call(
        _gather_kernel,
        grid_spec=pltpu.PrefetchScalarGridSpec(
            num_scalar_prefetch=2,
            grid=(m_pad // rows,),
            in_specs=[pl.BlockSpec(memory_space=pl.ANY)],
            out_specs=pl.BlockSpec((rows, d), lambda i, tok, nr: (i, 0)),
            scratch_shapes=[pltpu.SemaphoreType.DMA(())]),
        out_shape=jax.ShapeDtypeStruct((m_pad, d), x2.dtype),
        compiler_params=_cparams("arbitrary"),
        name="moe_gather",
    )(tok_sorted, n_rows, x2)


def _expert_kernel(blk_e_ref, nblk_ref, x_ref, wg_ref, wu_ref, w2_ref, o_ref, xb_ref, acc_ref):
    i = pl.program_id(0)
    f = pl.program_id(1)
    last = pl.num_programs(1) - 1
    live = i < nblk_ref[0]

    @pl.when(live & (f == 0))
    def _():
        xb_ref[...] = x_ref[...].astype(BF16)
        acc_ref[...] = jnp.zeros_like(acc_ref)

    @pl.when(live)
    def _():
        xb = xb_ref[...]
        gate = _dot(xb, wg_ref[0])
        up = _dot(xb, wu_ref[0])
        act = (gate * _sigmoid(gate) * up).astype(BF16)
        acc_ref[...] += _dot(act, w2_ref[0])

    @pl.when(live & (f == last))
    def _():
        o_ref[...] = acc_ref[...]

    @pl.when(jnp.logical_not(live) & (f == last))
    def _():
        o_ref[...] = jnp.zeros_like(o_ref)


def _expert_ffn(blk_e, n_blk_real, xs, w13, w2):
    m_pad, d = xs.shape
    ff = w2.shape[1]
    bm, tf = MOE_ROW_TILE, MOE_COL_TILE
    nf = ff // tf
    n_blk = m_pad // bm

    def row_map(i, f, be, nb):
        return (jnp.minimum(i, nb[0] - 1), 0)

    def fcol(i, f, nb):
        return jnp.where(i < nb[0], f, nf - 1)

    def e_of(i, be, nb):
        return be[jnp.minimum(i, nb[0] - 1)]

    return pl.pallas_call(
        _expert_kernel,
        grid_spec=pltpu.PrefetchScalarGridSpec(
            num_scalar_prefetch=2,
            grid=(n_blk, nf),
            in_specs=[pl.BlockSpec((bm, d), row_map),
                      pl.BlockSpec((1, d, tf), lambda i, f, be, nb: (e_of(i, be, nb), 0, fcol(i, f, nb))),
                      pl.BlockSpec((1, d, tf), lambda i, f, be, nb: (e_of(i, be, nb), 0, fcol(i, f, nb) + nf)),
                      pl.BlockSpec((1, tf, d), lambda i, f, be, nb: (e_of(i, be, nb), fcol(i, f, nb), 0))],
            out_specs=pl.BlockSpec((bm, d), lambda i, f, be, nb: (i, 0)),
            scratch_shapes=[pltpu.VMEM((bm, d), BF16), pltpu.VMEM((bm, d), F32)]),
        out_shape=jax.ShapeDtypeStruct((m_pad, d), F32),
        compiler_params=_cparams("arbitrary", "arbitrary"),
        name="moe_experts",
    )(blk_e, n_blk_real, xs, w13, w13, w2)


def _combine_kernel(d1_ref, d2_ref, y_hbm, x_ref, gate_ref, g_ref, b_ref, o_ref, y1_ref, y2_ref, sem1, sem2, *, alpha):
    i = pl.program_id(0)
    rows = x_ref.shape[0]
    base = i * rows

    def issue(r, c):
        _row_copy(y_hbm, d1_ref[base + r], y1_ref, r, sem1).start()
        _row_copy(y_hbm, d2_ref[base + r], y2_ref, r, sem2).start()
        return c

    lax.fori_loop(0, rows, issue, 0)

    def drain(r, c):
        _row_copy(y_hbm, 0, y1_ref, r, sem1).wait()
        _row_copy(y_hbm, 0, y2_ref, r, sem2).wait()
        return c

    lax.fori_loop(0, rows, drain, 0)
    gates = gate_ref[...]
    moe = gates[:, 0:1] * y1_ref[...] + gates[:, 1:2] * y2_ref[...]
    o_ref[...] = _layer_norm(alpha * x_ref[...] + moe, g_ref[...], b_ref[...])


def _combine_ln(dest1, dest2, ys, x2, gates, g, b, alpha):
    n, d = x2.shape
    rows = GATHER_ROWS
    return pl.pallas_call(
        functools.partial(_combine_kernel, alpha=alpha),
        grid_spec=pltpu.PrefetchScalarGridSpec(
            num_scalar_prefetch=2,
            grid=(n // rows,),
            in_specs=[pl.BlockSpec(memory_space=pl.ANY),
                      pl.BlockSpec((rows, d), lambda i, a, c: (i, 0)),
                      pl.BlockSpec((rows, LANES), lambda i, a, c: (i, 0)),
                      pl.BlockSpec((1, d), lambda i, a, c: (0, 0)),
                      pl.BlockSpec((1, d), lambda i, a, c: (0, 0))],
            out_specs=pl.BlockSpec((rows, d), lambda i, a, c: (i, 0)),
            scratch_shapes=[pltpu.VMEM((rows, d), F32), pltpu.VMEM((rows, d), F32),
                            pltpu.SemaphoreType.DMA(()), pltpu.SemaphoreType.DMA(())]),
        out_shape=jax.ShapeDtypeStruct((n, d), F32),
        compiler_params=_cparams("arbitrary"),
        name="moe_combine_ln",
    )(dest1, dest2, ys, x2, gates, g, b)


def _moe_ln(x2, router_w, w13, w2, g, b, alpha):
    n, d = x2.shape
    bm = MOE_ROW_TILE
    rw_pad = jnp.zeros((d, LANES), F32).at[:, :N_EXPERTS].set(router_w).astype(BF16)
    tri = jnp.asarray(np.tril(np.ones((ROW_TILE, ROW_TILE), np.float32), -1), BF16)
    info_i, info_f, cnt = _router(x2, rw_pad, tri)
    counts = cnt[0, :N_EXPERTS].astype(jnp.int32)
    pcounts = (counts + bm - 1) // bm * bm
    pend = jnp.cumsum(pcounts)
    pstart = pend - pcounts
    dest1 = pstart[info_i[:, 0]] + info_i[:, 2]
    dest2 = pstart[info_i[:, 1]] + info_i[:, 3]
    m_pad = 2 * n + N_EXPERTS * bm
    tok = jnp.arange(n, dtype=jnp.int32)
    tok_sorted = jnp.zeros((m_pad,), jnp.int32).at[dest1].set(tok).at[dest2].set(tok)
    n_blk = m_pad // bm
    blk_row = jnp.arange(n_blk, dtype=jnp.int32) * bm
    blk_e = jnp.minimum(jnp.sum((pend[None, :] <= blk_row[:, None]).astype(jnp.int32), axis=1), N_EXPERTS - 1)
    n_rows = pend[-1:].astype(jnp.int32)
    xs = _gather_rows(tok_sorted, n_rows, x2)
    ys = _expert_ffn(blk_e, n_rows // bm, xs, w13, w2)
    return _combine_ln(dest1.astype(jnp.int32), dest2.astype(jnp.int32), ys, x2, info_f, g, b, alpha)


def _block_diag(blocks):
    g, r, c = blocks.shape
    out = jnp.zeros((g * r, g * c), blocks.dtype)
    for k in range(g):
        out = out.at[k * r:(k + 1) * r, k * c:(k + 1) * c].set(blocks[k])
    return out


def _pack_w_in(w_in, d):
    pw = nw = sw = cw = d // 4
    dh = NSA_DH
    o = np.cumsum([0, pw, nw, 6 * dh, 3 * NSA_HEADS, 2 * sw, 2 * cw])
    a, q, kv, gl, uv, ag = (w_in[:, o[k]:o[k + 1]] for k in range(6))
    gl_pad = jnp.zeros((d, LANES), w_in.dtype).at[:, :gl.shape[1]].set(gl)
    packed = jnp.concatenate([ag, uv, a, kv[:, :2 * dh], gl_pad, q * (dh ** -0.5), kv[:, 2 * dh:]], axis=1)
    return packed.astype(BF16)


def _pack_compress(pe_k, w1k, pe_v, w1v, w2k, w2v):
    dh = NSA_DH
    half = CMP_STRIDE
    hid = w1k.shape[1]

    def halves(pe, w1):
        w = w1.reshape(CMP_BLOCK, dh, hid)
        return (pe[:half], pe[half:]), (w[:half], w[half:])

    (pka, pkb), (wka, wkb) = halves(pe_k, w1k)
    (pva, pvb), (wva, wvb) = halves(pe_v, w1v)
    zeros = jnp.zeros((half, dh, hid), w1k.dtype)

    def interleave_w(wk, wv):
        top = jnp.concatenate([wk, zeros], axis=2)
        bot = jnp.concatenate([zeros, wv], axis=2)
        return jnp.concatenate([top, bot], axis=1).reshape(half * 2 * dh, 2 * hid).astype(BF16)

    def interleave_pe(pk, pv):
        return jnp.concatenate([pk, pv], axis=1).reshape(1, half * 2 * dh)

    w2_bd = _block_diag(jnp.stack([w2k, w2v])).astype(BF16)
    return (interleave_pe(pka, pva), interleave_pe(pkb, pvb), interleave_w(wka, wva), interleave_w(wkb, wvb), w2_bd)


def _importance_matrix(t):
    n_chunk = t // CMP_STRIDE
    per = CMP_BLOCK // CMP_STRIDE
    nc = n_chunk - per + 1
    ns = t // SEL_BLOCK
    ratio = SEL_BLOCK // CMP_STRIDE
    lead = per - 1
    m = np.zeros((n_chunk, ns), np.float32)
    for s in range(ns):
        for j in list(range(ratio * s, ratio * s + ratio)) + [ratio + p + ratio * s for p in range(lead)]:
            n = j - lead
            if 0 <= n < nc:
                m[n, s] += 1.0
    return jnp.asarray(m, BF16)


def _block_expansion(t):
    tk = SEL_KEY_TILE
    key_blk = (np.arange(t) // SEL_BLOCK).reshape(t // tk, 1, tk)
    return jnp.asarray(key_blk == np.arange(t // SEL_BLOCK).reshape(1, -1, 1), BF16)


def kernel(x, mem, w_in, pool_w, pool_scale, cmp_pe_k, cmp_k_w1, cmp_k_w2, cmp_pe_v, cmp_v_w1, cmp_v_w2, sgu_ln_g, sgu_ln_b, sgu_w, sgu_b, conv_w, conv_b, conv_ln_g, conv_ln_b, conv_pw, w_out, ln1_g, ln1_b, xq_w, xkv_w, xo_w, ln2_g, ln2_b, ffn_w13, ffn_w2, router_w, exp_w13, exp_w2, ln3_g, ln3_b):
    b, t, d = x.shape
    depth = w_in.shape[0]
    alpha = float((2 * depth) ** 0.25)
    n = b * t
    m = mem.shape[1]
    sw = d // 4
    imp_w = _importance_matrix(t)
    expand = _block_expansion(t)
    row = lambda v: v.reshape(1, -1)
    x2 = x.reshape(n, d)
    mem2 = mem.reshape(b * m, d)
    for l in range(depth):
        ag, uv, pa, kvc, gl, q, kv = _in_proj(x2, _pack_w_in(w_in[l], d))
        pool_o = _pool_mixer(pa.reshape(b, t, -1), _block_diag(pool_w[l]).astype(BF16), row(pool_scale[l]))
        cmp_w = _pack_compress(cmp_pe_k[l], cmp_k_w1[l], cmp_pe_v[l], cmp_v_w1[l], cmp_k_w2[l], cmp_v_w2[l])
        kcv = _compress(kvc.reshape(b, t // CMP_STRIDE, CMP_STRIDE * 2 * NSA_DH), *cmp_w)
        nsa_o = _nsa_attention(q.reshape(b, t, -1), gl.reshape(b, t, -1), kcv, kv.reshape(b, t, -1), imp_w, expand)
        bias_full = jnp.repeat(sgu_b[l].T, sw // SGU_GROUPS, axis=1)
        sgu_o = _sgu_mixer(uv, row(sgu_ln_g[l]), row(sgu_ln_b[l]), sgu_w[l], bias_full)
        conv_o = _conv_mixer(ag.reshape(b, t, -1), conv_w[l], row(conv_b[l]), row(conv_ln_g[l]),
                             row(conv_ln_b[l]), conv_pw[l].astype(BF16))
        parts = [pool_o.reshape(n, -1), nsa_o.reshape(n, -1), sgu_o, conv_o.reshape(n, -1)]
        x2 = _out_proj_ln(parts, x2, w_out[l].astype(BF16), row(ln1_g[l]), row(ln1_b[l]), alpha)
        kvm = _matmul(mem2, xkv_w[l].astype(BF16), BF16)
        x2 = _xattn_ln(x2.reshape(b, t, d), kvm.reshape(b, m, 2 * d), xq_w[l].astype(BF16),
                       xo_w[l].astype(BF16), row(ln2_g[l]), row(ln2_b[l]), alpha).reshape(n, d)
        if l % 2 == 0:
            x2 = _ffn_ln(x2, ffn_w13[l // 2].astype(BF16), ffn_w2[l // 2].astype(BF16),
                         row(ln3_g[l]), row(ln3_b[l]), alpha)
        else:
            x2 = _moe_ln(x2, router_w[l // 2], exp_w13[l // 2].astype(BF16), exp_w2[l // 2].astype(BF16),
                         row(ln3_g[l]), row(ln3_b[l]), alpha)
    return x2.reshape(b, t, d)
```

```python
import functools

import numpy as np
import jax
import jax.numpy as jnp
from jax import lax
from jax.experimental import pallas as pl
from jax.experimental.pallas import tpu as pltpu

F32 = jnp.float32
BF16 = jnp.bfloat16

POOL_GROUPS = 4
POOL_WINDOWS = (2, 4, 8, 16)
POOL_HALO = 16
NSA_HEADS = 4
NSA_DH = 64
CMP_BLOCK = 32
CMP_STRIDE = 16
SEL_BLOCK = 64
SEL_TOPK = 16
WINDOW = 512
Q_BLOCK = 128
VT_ROWS = 80
FORCE_SCORE = 1e4
SGU_GROUPS = 4
SGU_CHUNK = 128
CONV_WIDTH = 31
CONV_HALO = 32
X_HEADS = 4
N_EXPERTS = 8
LN_EPS = 1e-5
NEG_INF = -1e30

LANES = 128
VMEM_LIMIT = 56 * 1024 * 1024
ROW_TILE = 512
SEL_KEY_TILE = 512
SEL_SUBTILES = 4
FFN_ROW_TILE = 1024
FFN_COL_TILE = 256
MOE_ROW_TILE = 1024
MOE_COL_TILE = 512
GATHER_ROWS = 256
DMA_UNROLL = 16


def _cparams(*sem):
    return pltpu.CompilerParams(dimension_semantics=sem, vmem_limit_bytes=VMEM_LIMIT)


def _layer_norm(x, g, b):
    mu = jnp.mean(x, axis=-1, keepdims=True)
    xc = x - mu
    var = jnp.mean(xc * xc, axis=-1, keepdims=True)
    return xc * lax.rsqrt(var + LN_EPS) * g + b


def _sigmoid(x):
    return 1.0 / (1.0 + jnp.exp(-x))


def _gelu_tanh(x):
    c = np.float32(np.sqrt(2.0 / np.pi))
    return 0.5 * x * (1.0 + jnp.tanh(c * (x + np.float32(0.044715) * (x * x * x))))


def _dot(a, b):
    return jnp.dot(a, b, preferred_element_type=F32)


def _dot_nt(a, b):
    return lax.dot_general(a, b, (((1,), (1,)), ((), ())), preferred_element_type=F32)


def _iota(shape, dim):
    return lax.broadcasted_iota(jnp.int32, shape, dim)


_IN_COLS = (("ag", 512, F32), ("uv", 512, F32), ("pool", 256, F32), ("kvc", 128, F32),
            ("gl", 128, F32), ("q", 256, BF16), ("kv", 256, BF16))


def _in_proj_kernel(x_ref, w_ref, *out_refs):
    xb = x_ref[...].astype(BF16)
    off = 0
    for (_, width, dt), o_ref in zip(_IN_COLS, out_refs):
        o_ref[...] = _dot(xb, w_ref[:, off:off + width]).astype(dt)
        off += width


def _in_proj(x2, w_packed):
    n, d = x2.shape
    tm = ROW_TILE
    tot = sum(w for _, w, _ in _IN_COLS)
    return pl.pallas_call(
        _in_proj_kernel,
        grid=(n // tm,),
        in_specs=[pl.BlockSpec((tm, d), lambda i: (i, 0)),
                  pl.BlockSpec((d, tot), lambda i: (0, 0))],
        out_specs=[pl.BlockSpec((tm, w), lambda i: (i, 0)) for _, w, _ in _IN_COLS],
        out_shape=[jax.ShapeDtypeStruct((n, w), dt) for _, w, dt in _IN_COLS],
        compiler_params=_cparams("parallel"),
        name="in_proj",
    )(x2, w_packed)


def _pool_kernel(a_ref, halo_ref, w_ref, scale_ref, o_ref):
    i = pl.program_id(1)
    tt = a_ref.shape[1]
    a = a_ref[0]
    halo = jnp.where(i > 0, halo_ref[0], 0.0)
    ext = jnp.concatenate([halo, a], axis=0)
    s2 = ext + pltpu.roll(ext, 1, 0)
    s4 = s2 + pltpu.roll(s2, 2, 0)
    s8 = s4 + pltpu.roll(s4, 4, 0)
    s16 = s8 + pltpu.roll(s8, 8, 0)
    grp = _iota(a.shape, 1) // (a.shape[1] // POOL_GROUPS)
    pos = i * tt + _iota(a.shape, 0)
    h = POOL_HALO
    s = jnp.where(grp == 0, s2[h:], jnp.where(grp == 1, s4[h:], jnp.where(grp == 2, s8[h:], s16[h:])))
    win = jnp.where(grp == 0, 2, jnp.where(grp == 1, 4, jnp.where(grp == 2, 8, 16)))
    cnt = jnp.minimum(pos + 1, win).astype(F32)
    d = s / cnt - a
    y = _dot(d.astype(BF16), w_ref[...]) * scale_ref[...]
    o_ref[0] = y.astype(o_ref.dtype)


def _pool_mixer(a3, w_bd, scale):
    b, t, c = a3.shape
    tt = ROW_TILE
    hb = tt // POOL_HALO
    return pl.pallas_call(
        _pool_kernel,
        grid=(b, t // tt),
        in_specs=[pl.BlockSpec((1, tt, c), lambda bi, i: (bi, i, 0)),
                  pl.BlockSpec((1, POOL_HALO, c), lambda bi, i: (bi, jnp.maximum(i * hb - 1, 0), 0)),
                  pl.BlockSpec((c, c), lambda bi, i: (0, 0)),
                  pl.BlockSpec((1, c), lambda bi, i: (0, 0))],
        out_specs=pl.BlockSpec((1, tt, c), lambda bi, i: (bi, i, 0)),
        out_shape=jax.ShapeDtypeStruct((b, t, c), BF16),
        compiler_params=_cparams("parallel", "parallel"),
        name="pool_mixer",
    )(a3, a3, w_bd, scale)


def _conv_kernel(ag_ref, halo_ref, wdw_ref, bdw_ref, g_ref, b_ref, wpw_ref, o_ref):
    i = pl.program_id(1)
    tt = ag_ref.shape[1]
    cw = ag_ref.shape[2] // 2
    ag = jnp.concatenate([halo_ref[0], ag_ref[0]], axis=0)
    h = ag[:, :cw] * _sigmoid(ag[:, cw:])
    row = _iota(h.shape, 0)
    h = jnp.where((row >= CONV_HALO) | (i > 0), h, 0.0)
    n_ext = tt + CONV_HALO
    lead = CONV_HALO - (CONV_WIDTH - 1)
    acc = jnp.zeros((tt, cw), F32)
    for r in range(8):
        hr = h if r == 0 else pltpu.roll(h, n_ext - r, 0)
        for k in range(CONV_WIDTH):
            if (lead + k) % 8 == r:
                off = lead + k - r
                acc = acc + hr[off:off + tt] * wdw_ref[k:k + 1, :]
    y = acc + bdw_ref[...]
    y = _layer_norm(y, g_ref[...], b_ref[...])
    y = y * _sigmoid(y)
    o_ref[0] = _dot(y.astype(BF16), wpw_ref[...]).astype(o_ref.dtype)


def _conv_mixer(ag3, w_dw, b_dw, ln_g, ln_b, w_pw):
    b, t, c2 = ag3.shape
    cw = c2 // 2
    tt = ROW_TILE
    hb = tt // CONV_HALO
    full = lambda bi, i: (0, 0)
    return pl.pallas_call(
        _conv_kernel,
        grid=(b, t // tt),
        in_specs=[pl.BlockSpec((1, tt, c2), lambda bi, i: (bi, i, 0)),
                  pl.BlockSpec((1, CONV_HALO, c2), lambda bi, i: (bi, jnp.maximum(i * hb - 1, 0), 0)),
                  pl.BlockSpec((CONV_WIDTH, cw), full),
                  pl.BlockSpec((1, cw), full), pl.BlockSpec((1, cw), full), pl.BlockSpec((1, cw), full),
                  pl.BlockSpec((cw, cw), full)],
        out_specs=pl.BlockSpec((1, tt, cw), lambda bi, i: (bi, i, 0)),
        out_shape=jax.ShapeDtypeStruct((b, t, cw), BF16),
        compiler_params=_cparams("parallel", "parallel"),
        name="conv_mixer",
    )(ag3, ag3, w_dw, b_dw, ln_g, ln_b, w_pw)


def _sgu_kernel(uv_ref, g_ref, b_ref, ws_ref, bs_ref, o_ref):
    tt = uv_ref.shape[0]
    sw = uv_ref.shape[1] // 2
    gw = sw // SGU_GROUPS
    act = _gelu_tanh(uv_ref[...])
    u = act[:, :sw]
    v = _layer_norm(act[:, sw:], g_ref[...], b_ref[...]).astype(BF16)
    causal = _iota((SGU_CHUNK, SGU_CHUNK), 1) <= _iota((SGU_CHUNK, SGU_CHUNK), 0)
    ws = [jnp.where(causal, ws_ref[g], 0.0).astype(BF16) for g in range(SGU_GROUPS)]
    lane_grp = _iota((SGU_CHUNK, sw), 1) // gw
    for ch in range(tt // SGU_CHUNK):
        rows = slice(ch * SGU_CHUNK, (ch + 1) * SGU_CHUNK)
        vc = v[rows]
        mix = jnp.zeros((SGU_CHUNK, sw), F32)
        for g in range(SGU_GROUPS):
            mix = jnp.where(lane_grp == g, _dot(ws[g], vc), mix)
        o_ref[rows, :] = (u[rows] * (mix + bs_ref[...])).astype(o_ref.dtype)


def _sgu_mixer(uv2, ln_g, ln_b, w_s, bias_full):
    n, c2 = uv2.shape
    sw = c2 // 2
    tm = ROW_TILE
    return pl.pallas_call(
        _sgu_kernel,
        grid=(n // tm,),
        in_specs=[pl.BlockSpec((tm, c2), lambda i: (i, 0)),
                  pl.BlockSpec((1, sw), lambda i: (0, 0)), pl.BlockSpec((1, sw), lambda i: (0, 0)),
                  pl.BlockSpec((SGU_GROUPS, SGU_CHUNK, SGU_CHUNK), lambda i: (0, 0, 0)),
                  pl.BlockSpec((SGU_CHUNK, sw), lambda i: (0, 0))],
        out_specs=pl.BlockSpec((tm, sw), lambda i: (i, 0)),
        out_shape=jax.ShapeDtypeStruct((n, sw), BF16),
        compiler_params=_cparams("parallel"),
        name="sgu_mixer",
    )(uv2, ln_g, ln_b, w_s, bias_full)


def _compress_kernel(rc_ref, pea_ref, peb_ref, wa_ref, wb_ref, w2_ref, o_ref):
    x = rc_ref[0]
    nch = x.shape[0]
    first = _dot((x + pea_ref[...]).astype(BF16), wa_ref[...])
    second = _dot((x + peb_ref[...]).astype(BF16), wb_ref[...])
    hid = _gelu_tanh(first + pltpu.roll(second, nch - 1, 0))
    o_ref[0] = _dot(hid.astype(BF16), w2_ref[...]).astype(o_ref.dtype)


def _compress(rc3, pe_a, pe_b, w_a, w_b, w2_bd):
    b, nch, width = rc3.shape
    hid2 = w_a.shape[1]
    full = lambda bi: (0, 0)
    return pl.pallas_call(
        _compress_kernel,
        grid=(b,),
        in_specs=[pl.BlockSpec((1, nch, width), lambda bi: (bi, 0, 0)),
                  pl.BlockSpec((1, width), full), pl.BlockSpec((1, width), full),
                  pl.BlockSpec((width, hid2), full), pl.BlockSpec((width, hid2), full),
                  pl.BlockSpec((hid2, 2 * NSA_DH), full)],
        out_specs=pl.BlockSpec((1, nch, 2 * NSA_DH), lambda bi: (bi, 0, 0)),
        out_shape=jax.ShapeDtypeStruct((b, nch, 2 * NSA_DH), BF16),
        compiler_params=_cparams("parallel"),
        name="nsa_compress",
    )(rc3, pe_a, pe_b, w_a, w_b, w2_bd)


def _nsa_kernel(q_ref, gl_ref, kc_ref, vct_ref, kv_ref, vst_ref, vwt_ref, impt_ref, o_ref, bias_ref, *, n_sel):
    i = pl.program_id(1)
    qb = q_ref.shape[1]
    ncp = kc_ref.shape[1]
    ns = impt_ref.shape[0]
    dh = NSA_DH
    nh = NSA_HEADS
    tk = SEL_KEY_TILE
    cpt = tk // qb
    q0 = i * qb
    qt = q_ref[0].astype(F32).T
    q_t = jnp.concatenate([qt[h * dh:(h + 1) * dh] for h in range(nh)], axis=1).astype(BF16)
    tq = q0 + _iota((1, qb), 1)

    def heads(x):
        return jnp.concatenate([x] * nh, axis=1)

    kc = kc_ref[0][:, :dh]
    m_c = (_iota((ncp, qb), 0) * CMP_STRIDE + (CMP_BLOCK - 1)) <= tq
    s = _dot(kc, q_t) + heads(jnp.where(m_c, 0.0, NEG_INF))
    p = jnp.exp(s - jnp.max(s, axis=0, keepdims=True))
    l = jnp.sum(p, axis=0, keepdims=True)
    p = p * jnp.where(heads(tq) >= CMP_BLOCK - 1, 1.0 / l, 0.0)
    psum = p[:, 0:qb]
    for h in range(1, nh):
        psum = psum + p[:, h * qb:(h + 1) * qb]
    o_c = _dot(vct_ref[0], p.astype(BF16))

    p_hi = psum.astype(BF16)
    r1 = psum - p_hi.astype(F32)
    p_mid = r1.astype(BF16)
    p_lo = (r1 - p_mid.astype(F32)).astype(BF16)
    imp_w = impt_ref[...]
    imp = _dot(imp_w, p_hi) + _dot(imp_w, p_mid) + _dot(imp_w, p_lo)
    sid = _iota((ns, qb), 0)
    cur = tq // SEL_BLOCK
    forced = (sid == 0) | (sid == cur) | (sid == cur - 1)
    score = jnp.where(sid <= cur, imp + jnp.where(forced, FORCE_SCORE, 0.0), -1.0)

    sid_f = sid.astype(F32)
    work = score
    chosen = jnp.zeros((ns, qb), F32)
    for _ in range(n_sel):
        mx = jnp.max(work, axis=0, keepdims=True)
        first = jnp.min(jnp.where(work == mx, sid_f, float(ns)), axis=0, keepdims=True)
        pick = sid_f == first
        chosen = jnp.where(pick, 1.0, chosen)
        work = jnp.where(pick, -2.0, work)
    bias_ref[...] = jnp.where((chosen > 0.5) & (score >= 0.0), 0.0, NEG_INF)

    nwc = WINDOW // qb + 1
    wk = nwc * qb
    start = pl.multiple_of(jnp.maximum(q0 - WINDOW, 0), qb)
    c0 = start // qb
    kw = kv_ref[0, pl.ds(start, wk), :][:, 2 * dh:3 * dh]
    kpos = start + _iota((wk, qb), 0)
    s = _dot(kw, q_t) + heads(jnp.where((kpos <= tq) & (kpos > tq - WINDOW), 0.0, NEG_INF))
    p = jnp.exp(s - jnp.max(s, axis=0, keepdims=True)).astype(BF16)
    vw_t = jnp.concatenate([vwt_ref[0, c0 + j] for j in range(nwc)], axis=1)
    acc_w = _dot(vw_t, p)
    o_w = acc_w[0:dh] * (1.0 / acc_w[dh:dh + 1])

    def sel_tile(kt, carry, causal):
        m_run, acc = carry
        k0 = pl.multiple_of(kt * tk, tk)
        ks = kv_ref[0, pl.ds(k0, tk), :][:, 0:dh]
        rows = bias_ref[pl.ds(pl.multiple_of(kt * (tk // SEL_BLOCK), 8), tk // SEL_BLOCK), :]
        sub = tk // SEL_SUBTILES
        scores = []
        for u in range(SEL_SUBTILES):
            blocks = range(u * sub // SEL_BLOCK, (u + 1) * sub // SEL_BLOCK)
            bias = jnp.concatenate([jnp.broadcast_to(rows[j:j + 1], (SEL_BLOCK, qb)) for j in blocks], axis=0)
            if causal:
                bias = bias + jnp.where((k0 + u * sub + _iota((sub, qb), 0)) <= tq, 0.0, NEG_INF)
            scores.append(_dot(ks[u * sub:(u + 1) * sub], q_t) + heads(bias))
        for u in range(SEL_SUBTILES):
            s = scores[u]
            m_new = jnp.maximum(m_run, jnp.max(s, axis=0, keepdims=True))
            alpha = jnp.exp(m_run - m_new)
            p = jnp.exp(s - m_new).astype(BF16)
            chunks = range(u * sub // qb, (u + 1) * sub // qb)
            vs_t = jnp.concatenate([vst_ref[0, kt * cpt + j] for j in chunks], axis=1)
            m_run, acc = m_new, alpha * acc + _dot(vs_t, p)
        return m_run, acc

    n_full = q0 // tk
    init = (jnp.full((1, nh * qb), NEG_INF, F32), jnp.zeros((vst_ref.shape[2], nh * qb), F32))
    carry = lax.fori_loop(0, n_full, functools.partial(sel_tile, causal=False), init)
    _, acc_s = sel_tile(n_full, carry, True)
    o_s = acc_s[0:dh] * (1.0 / acc_s[dh:dh + 1])

    g_t = _sigmoid(gl_ref[0]).T
    gate = lambda j: jnp.concatenate([g_t[3 * h + j:3 * h + j + 1] for h in range(nh)], axis=1)
    o = gate(0) * o_c + gate(1) * o_s + gate(2) * o_w
    pad = jnp.zeros((qb - dh, qb), F32)
    for h in range(nh):
        o_h = jnp.concatenate([o[:, h * qb:(h + 1) * qb], pad], axis=0).T
        o_ref[0, :, h * dh:(h + 1) * dh] = o_h[:, :dh].astype(o_ref.dtype)


def _value_chunks(v):
    b, t, dh = v.shape
    vt = v.reshape(b, t // Q_BLOCK, Q_BLOCK, dh).transpose(0, 1, 3, 2)
    ones = jnp.ones((b, t // Q_BLOCK, 1, Q_BLOCK), v.dtype)
    zeros = jnp.zeros((b, t // Q_BLOCK, VT_ROWS - dh - 1, Q_BLOCK), v.dtype)
    return jnp.concatenate([vt, ones, zeros], axis=2)


def _nsa_attention(q3, gl3, kcv3, kv3, imp_w):
    b, t, qw = q3.shape
    ncp = kcv3.shape[1]
    ns = t // SEL_BLOCK
    dh = NSA_DH
    n_sel = min(SEL_TOPK, ns)
    assert t >= WINDOW + Q_BLOCK and t % SEL_KEY_TILE == 0 and ns % 8 == 0
    vct = kcv3[:, :, dh:].transpose(0, 2, 1)
    vst = _value_chunks(kv3[:, :, dh:2 * dh])
    vwt = _value_chunks(kv3[:, :, 3 * dh:4 * dh])
    nchunk = t // Q_BLOCK
    return pl.pallas_call(
        functools.partial(_nsa_kernel, n_sel=n_sel),
        grid=(b, t // Q_BLOCK),
        in_specs=[pl.BlockSpec((1, Q_BLOCK, qw), lambda bi, i: (bi, i, 0)),
                  pl.BlockSpec((1, Q_BLOCK, LANES), lambda bi, i: (bi, i, 0)),
                  pl.BlockSpec((1, ncp, 2 * dh), lambda bi, i: (bi, 0, 0)),
                  pl.BlockSpec((1, dh, ncp), lambda bi, i: (bi, 0, 0)),
                  pl.BlockSpec((1, t, 4 * dh), lambda bi, i: (bi, 0, 0)),
                  pl.BlockSpec((1, nchunk, VT_ROWS, Q_BLOCK), lambda bi, i: (bi, 0, 0, 0)),
                  pl.BlockSpec((1, nchunk, VT_ROWS, Q_BLOCK), lambda bi, i: (bi, 0, 0, 0)),
                  pl.BlockSpec((ns, ncp), lambda bi, i: (0, 0))],
        out_specs=pl.BlockSpec((1, Q_BLOCK, qw), lambda bi, i: (bi, i, 0)),
        out_shape=jax.ShapeDtypeStruct((b, t, qw), BF16),
        scratch_shapes=[pltpu.VMEM((ns, Q_BLOCK), F32)],
        compiler_params=_cparams("parallel", "parallel"),
        name="nsa_attention",
    )(q3, gl3, kcv3, vct, kv3, vst, vwt, imp_w.T)


def _out_proj_kernel(p_ref, n_ref, s_ref, c_ref, x_ref, w_ref, g_ref, b_ref, o_ref, *, alpha):
    y = jnp.concatenate([p_ref[...], n_ref[...], s_ref[...], c_ref[...]], axis=-1)
    mix = _dot(y, w_ref[...])
    o_ref[...] = _layer_norm(alpha * x_ref[...] + mix, g_ref[...], b_ref[...])


def _out_proj_ln(parts, x2, w_out, g, b, alpha):
    n, d = x2.shape
    tm = ROW_TILE
    pw = parts[0].shape[1]
    return pl.pallas_call(
        functools.partial(_out_proj_kernel, alpha=alpha),
        grid=(n // tm,),
        in_specs=[pl.BlockSpec((tm, pw), lambda i: (i, 0)) for _ in parts]
        + [pl.BlockSpec((tm, d), lambda i: (i, 0)),
           pl.BlockSpec(w_out.shape, lambda i: (0, 0)),
           pl.BlockSpec((1, d), lambda i: (0, 0)), pl.BlockSpec((1, d), lambda i: (0, 0))],
        out_specs=pl.BlockSpec((tm, d), lambda i: (i, 0)),
        out_shape=jax.ShapeDtypeStruct((n, d), F32),
        compiler_params=_cparams("parallel"),
        name="out_proj_ln",
    )(*parts, x2, w_out, g, b)


def _matmul_kernel(a_ref, w_ref, o_ref):
    o_ref[...] = _dot(a_ref[...].astype(BF16), w_ref[...]).astype(o_ref.dtype)


def _matmul(a, w, out_dtype):
    n, k = a.shape
    m = w.shape[1]
    tm = min(ROW_TILE, n)
    return pl.pallas_call(
        _matmul_kernel,
        grid=(n // tm,),
        in_specs=[pl.BlockSpec((tm, k), lambda i: (i, 0)), pl.BlockSpec((k, m), lambda i: (0, 0))],
        out_specs=pl.BlockSpec((tm, m), lambda i: (i, 0)),
        out_shape=jax.ShapeDtypeStruct((n, m), out_dtype),
        compiler_params=_cparams("parallel"),
        name="mem_kv_proj",
    )(a, w)


def _xattn_kernel(x_ref, kv_ref, wq_ref, wo_ref, g_ref, b_ref, o_ref, *, alpha):
    x = x_ref[0]
    d = x.shape[1]
    dh = d // X_HEADS
    q = (_dot(x.astype(BF16), wq_ref[...]) * (dh ** -0.5)).astype(BF16)
    kv = kv_ref[0]
    outs = []
    for h in range(X_HEADS):
        k = kv[:, h * dh:(h + 1) * dh]
        v = kv[:, d + h * dh:d + (h + 1) * dh]
        s = _dot_nt(q[:, h * dh:(h + 1) * dh], k)
        p = jnp.exp(s - jnp.max(s, axis=-1, keepdims=True))
        p = p / jnp.sum(p, axis=-1, keepdims=True)
        outs.append(_dot(p.astype(BF16), v).astype(BF16))
    att = _dot(jnp.concatenate(outs, axis=-1), wo_ref[...])
    o_ref[0] = _layer_norm(alpha * x + att, g_ref[...], b_ref[...])


def _xattn_ln(x3, kv3, wq, wo, g, b, alpha):
    bsz, t, d = x3.shape
    m = kv3.shape[1]
    tt = ROW_TILE
    full = lambda bi, i: (0, 0)
    return pl.pallas_call(
        functools.partial(_xattn_kernel, alpha=alpha),
        grid=(bsz, t // tt),
        in_specs=[pl.BlockSpec((1, tt, d), lambda bi, i: (bi, i, 0)),
                  pl.BlockSpec((1, m, 2 * d), lambda bi, i: (bi, 0, 0)),
                  pl.BlockSpec((d, d), full), pl.BlockSpec((d, d), full),
                  pl.BlockSpec((1, d), full), pl.BlockSpec((1, d), full)],
        out_specs=pl.BlockSpec((1, tt, d), lambda bi, i: (bi, i, 0)),
        out_shape=jax.ShapeDtypeStruct((bsz, t, d), F32),
        compiler_params=_cparams("parallel", "parallel"),
        name="xattn_ln",
    )(x3, kv3, wq, wo, g, b)


def _ffn_kernel(x_ref, wg_ref, wu_ref, w2_ref, g_ref, b_ref, o_ref, xb_ref, acc_ref, *, alpha):
    f = pl.program_id(1)

    @pl.when(f == 0)
    def _():
        xb_ref[...] = x_ref[...].astype(BF16)
        acc_ref[...] = jnp.zeros_like(acc_ref)

    xb = xb_ref[...]
    gate = _dot(xb, wg_ref[...])
    up = _dot(xb, wu_ref[...])
    act = (gate * _sigmoid(gate) * up).astype(BF16)
    acc_ref[...] += _dot(act, w2_ref[...])

    @pl.when(f == pl.num_programs(1) - 1)
    def _():
        o_ref[...] = _layer_norm(alpha * x_ref[...] + acc_ref[...], g_ref[...], b_ref[...])


def _ffn_ln(x2, w13, w2, g, b, alpha):
    n, d = x2.shape
    ff = w2.shape[0]
    tm, tf = FFN_ROW_TILE, FFN_COL_TILE
    nf = ff // tf
    return pl.pallas_call(
        functools.partial(_ffn_kernel, alpha=alpha),
        grid=(n // tm, nf),
        in_specs=[pl.BlockSpec((tm, d), lambda i, f: (i, 0)),
                  pl.BlockSpec((d, tf), lambda i, f: (0, f)),
                  pl.BlockSpec((d, tf), lambda i, f: (0, f + nf)),
                  pl.BlockSpec((tf, d), lambda i, f: (f, 0)),
                  pl.BlockSpec((1, d), lambda i, f: (0, 0)), pl.BlockSpec((1, d), lambda i, f: (0, 0))],
        out_specs=pl.BlockSpec((tm, d), lambda i, f: (i, 0)),
        out_shape=jax.ShapeDtypeStruct((n, d), F32),
        scratch_shapes=[pltpu.VMEM((tm, d), BF16), pltpu.VMEM((tm, d), F32)],
        compiler_params=_cparams("parallel", "arbitrary"),
        name="ffn_ln",
    )(x2, w13, w13, w2, g, b)


def _router_kernel(x_ref, rw_ref, tri_ref, ii_ref, if_ref, cnt_ref, carry_ref):
    @pl.when(pl.program_id(0) == 0)
    def _():
        carry_ref[...] = jnp.zeros_like(carry_ref)

    logits = _dot(x_ref[...].astype(BF16), rw_ref[...])
    lane = _iota(logits.shape, 1)
    lf = lane.astype(F32)
    low = np.float32(-3e38)
    lg = jnp.where(lane < N_EXPERTS, logits, low)
    m1 = jnp.max(lg, axis=-1, keepdims=True)
    i1 = jnp.min(jnp.where(lg == m1, lf, float(LANES)), axis=-1, keepdims=True)
    lg2 = jnp.where(lf == i1, low, lg)
    m2 = jnp.max(lg2, axis=-1, keepdims=True)
    i2 = jnp.min(jnp.where(lg2 == m2, lf, float(LANES)), axis=-1, keepdims=True)
    e = jnp.exp(m2 - m1)
    g1 = 1.0 / (1.0 + e)
    g2 = e / (1.0 + e)
    hit1 = lf == i1
    hit2 = lf == i2
    onehot = jnp.where(hit1 | hit2, 1.0, 0.0)
    ranks = _dot(tri_ref[...], onehot.astype(BF16)) + carry_ref[...]
    r1 = jnp.sum(jnp.where(hit1, ranks, 0.0), axis=-1, keepdims=True)
    r2 = jnp.sum(jnp.where(hit2, ranks, 0.0), axis=-1, keepdims=True)
    carry_ref[...] += jnp.sum(onehot, axis=0, keepdims=True)
    info = jnp.where(lane == 0, i1, jnp.where(lane == 1, i2, jnp.where(lane == 2, r1, jnp.where(lane == 3, r2, 0.0))))
    ii_ref[...] = info.astype(jnp.int32)
    if_ref[...] = jnp.where(lane == 0, g1, jnp.where(lane == 1, g2, 0.0))
    cnt_ref[...] = carry_ref[...]


def _router(x2, rw_pad, tri):
    n, d = x2.shape
    tm = ROW_TILE
    return pl.pallas_call(
        _router_kernel,
        grid=(n // tm,),
        in_specs=[pl.BlockSpec((tm, d), lambda i: (i, 0)),
                  pl.BlockSpec((d, LANES), lambda i: (0, 0)),
                  pl.BlockSpec((tm, tm), lambda i: (0, 0))],
        out_specs=[pl.BlockSpec((tm, LANES), lambda i: (i, 0)),
                   pl.BlockSpec((tm, LANES), lambda i: (i, 0)),
                   pl.BlockSpec((1, LANES), lambda i: (0, 0))],
        out_shape=[jax.ShapeDtypeStruct((n, LANES), jnp.int32),
                   jax.ShapeDtypeStruct((n, LANES), F32),
                   jax.ShapeDtypeStruct((1, LANES), F32)],
        scratch_shapes=[pltpu.VMEM((1, LANES), F32)],
        compiler_params=_cparams("arbitrary"),
        name="moe_router",
    )(x2, rw_pad, tri)


def _row_copy(src_hbm, row, dst_ref, r, sem):
    return pltpu.make_async_copy(src_hbm.at[pl.ds(row, 1)], dst_ref.at[pl.ds(r, 1)], sem)


def _gather_kernel(tok_ref, nrows_ref, x_hbm, o_ref, sem):
    i = pl.program_id(0)
    rows = o_ref.shape[0]
    base = i * rows

    @pl.when(base < nrows_ref[0])
    def _():
        def issue(r, c):
            _row_copy(x_hbm, tok_ref[base + r], o_ref, r, sem).start()
            return c

        lax.fori_loop(0, rows, issue, 0, unroll=DMA_UNROLL)

        def drain(r, c):
            _row_copy(x_hbm, 0, o_ref, r, sem).wait()
            return c

        lax.fori_loop(0, rows, drain, 0, unroll=True)

    @pl.when(base >= nrows_ref[0])
    def _():
        o_ref[...] = jnp.zeros_like(o_ref)


def _gather_rows(tok_sorted, n_rows, x2):
    m_pad = tok_sorted.shape[0]
    d = x2.shape[1]
    rows = GATHER_ROWS
    return pl.pallas_call(
        _gather_kernel,
        grid_spec=pltpu.PrefetchScalarGridSpec(
            num_scalar_prefetch=2,
            grid=(m_pad // rows,),
            in_specs=[pl.BlockSpec(memory_space=pl.ANY)],
            out_specs=pl.BlockSpec((rows, d), lambda i, tok, nr: (i, 0)),
            scratch_shapes=[pltpu.SemaphoreType.DMA(())]),
        out_shape=jax.ShapeDtypeStruct((m_pad, d), x2.dtype),
        compiler_params=_cparams("arbitrary"),
        name="moe_gather",
    )(tok_sorted, n_rows, x2)


def _expert_kernel(blk_e_ref, nblk_ref, x_ref, wg_ref, wu_ref, w2_ref, o_ref, xb_ref, acc_ref):
    i = pl.program_id(0)
    f = pl.program_id(1)
    last = pl.num_programs(1) - 1
    live = i < nblk_ref[0]

    @pl.when(live & (f == 0))
    def _():
        xb_ref[...] = x_ref[...].astype(BF16)
        acc_ref[...] = jnp.zeros_like(acc_ref)

    @pl.when(live)
    def _():
        xb = xb_ref[...]
        gate = _dot(xb, wg_ref[0])
        up = _dot(xb, wu_ref[0])
        act = (gate * _sigmoid(gate) * up).astype(BF16)
        acc_ref[...] += _dot(act, w2_ref[0])

    @pl.when(live & (f == last))
    def _():
        o_ref[...] = acc_ref[...]

    @pl.when(jnp.logical_not(live) & (f == last))
    def _():
        o_ref[...] = jnp.zeros_like(o_ref)


def _expert_ffn(blk_e, n_blk_real, xs, w13, w2):
    m_pad, d = xs.shape
    ff = w2.shape[1]
    bm, tf = MOE_ROW_TILE, MOE_COL_TILE
    nf = ff // tf
    n_blk = m_pad // bm

    def row_map(i, f, be, nb):
        return (jnp.minimum(i, nb[0] - 1), 0)

    def fcol(i, f, nb):
        return jnp.where(i < nb[0], f, nf - 1)

    def e_of(i, be, nb):
        return be[jnp.minimum(i, nb[0] - 1)]

    return pl.pallas_call(
        _expert_kernel,
        grid_spec=pltpu.PrefetchScalarGridSpec(
            num_scalar_prefetch=2,
            grid=(n_blk, nf),
            in_specs=[pl.BlockSpec((bm, d), row_map),
                      pl.BlockSpec((1, d, tf), lambda i, f, be, nb: (e_of(i, be, nb), 0, fcol(i, f, nb))),
                      pl.BlockSpec((1, d, tf), lambda i, f, be, nb: (e_of(i, be, nb), 0, fcol(i, f, nb) + nf)),
                      pl.BlockSpec((1, tf, d), lambda i, f, be, nb: (e_of(i, be, nb), fcol(i, f, nb), 0))],
            out_specs=pl.BlockSpec((bm, d), lambda i, f, be, nb: (i, 0)),
            scratch_shapes=[pltpu.VMEM((bm, d), BF16), pltpu.VMEM((bm, d), F32)]),
        out_shape=jax.ShapeDtypeStruct((m_pad, d), F32),
        compiler_params=_cparams("arbitrary", "arbitrary"),
        name="moe_experts",
    )(blk_e, n_blk_real, xs, w13, w13, w2)


def _combine_kernel(d1_ref, d2_ref, y_hbm, x_ref, gate_ref, g_ref, b_ref, o_ref, y1_ref, y2_ref, sem1, sem2, *, alpha):
    i = pl.program_id(0)
    rows = x_ref.shape[0]
    base = i * rows

    def issue(r, c):
        _row_copy(y_hbm, d1_ref[base + r], y1_ref, r, sem1).start()
        _row_copy(y_hbm, d2_ref[base + r], y2_ref, r, sem2).start()
        return c

    lax.fori_loop(0, rows, issue, 0, unroll=DMA_UNROLL)

    def drain(r, c):
        _row_copy(y_hbm, 0, y1_ref, r, sem1).wait()
        _row_copy(y_hbm, 0, y2_ref, r, sem2).wait()
        return c

    lax.fori_loop(0, rows, drain, 0, unroll=True)
    gates = gate_ref[...]
    moe = gates[:, 0:1] * y1_ref[...] + gates[:, 1:2] * y2_ref[...]
    o_ref[...] = _layer_norm(alpha * x_ref[...] + moe, g_ref[...], b_ref[...])


def _combine_ln(dest1, dest2, ys, x2, gates, g, b, alpha):
    n, d = x2.shape
    rows = GATHER_ROWS
    return pl.pallas_call(
        functools.partial(_combine_kernel, alpha=alpha),
        grid_spec=pltpu.PrefetchScalarGridSpec(
            num_scalar_prefetch=2,
            grid=(n // rows,),
            in_specs=[pl.BlockSpec(memory_space=pl.ANY),
                      pl.BlockSpec((rows, d), lambda i, a, c: (i, 0)),
                      pl.BlockSpec((rows, LANES), lambda i, a, c: (i, 0)),
                      pl.BlockSpec((1, d), lambda i, a, c: (0, 0)),
                      pl.BlockSpec((1, d), lambda i, a, c: (0, 0))],
            out_specs=pl.BlockSpec((rows, d), lambda i, a, c: (i, 0)),
            scratch_shapes=[pltpu.VMEM((rows, d), F32), pltpu.VMEM((rows, d), F32),
                            pltpu.SemaphoreType.DMA(()), pltpu.SemaphoreType.DMA(())]),
        out_shape=jax.ShapeDtypeStruct((n, d), F32),
        compiler_params=_cparams("arbitrary"),
        name="moe_combine_ln",
    )(dest1, dest2, ys, x2, gates, g, b)


def _moe_ln(x2, router_w, w13_all, w2_all, layer, g, b, alpha):
    n, d = x2.shape
    bm = MOE_ROW_TILE
    rw_pad = jnp.zeros((d, LANES), F32).at[:, :N_EXPERTS].set(router_w).astype(BF16)
    tri = jnp.asarray(np.tril(np.ones((ROW_TILE, ROW_TILE), np.float32), -1), BF16)
    info_i, info_f, cnt = _router(x2, rw_pad, tri)
    counts = cnt[0, :N_EXPERTS].astype(jnp.int32)
    pcounts = (counts + bm - 1) // bm * bm
    pend = jnp.cumsum(pcounts)
    pstart = pend - pcounts
    dest1 = pstart[info_i[:, 0]] + info_i[:, 2]
    dest2 = pstart[info_i[:, 1]] + info_i[:, 3]
    m_pad = 2 * n + N_EXPERTS * bm
    tok = jnp.arange(n, dtype=jnp.int32)
    tok_sorted = jnp.zeros((m_pad,), jnp.int32).at[jnp.concatenate([dest1, dest2])].set(jnp.concatenate([tok, tok]))
    n_blk = m_pad // bm
    blk_row = jnp.arange(n_blk, dtype=jnp.int32) * bm
    blk_e = jnp.minimum(jnp.sum((pend[None, :] <= blk_row[:, None]).astype(jnp.int32), axis=1), N_EXPERTS - 1)
    n_rows = pend[-1:].astype(jnp.int32)
    xs = _gather_rows(tok_sorted, n_rows, x2)
    ys = _expert_ffn(blk_e + layer * N_EXPERTS, n_rows // bm, xs, w13_all, w2_all)
    return _combine_ln(dest1.astype(jnp.int32), dest2.astype(jnp.int32), ys, x2, info_f, g, b, alpha)


def _block_diag(blocks):
    g, r, c = blocks.shape
    out = jnp.zeros((g * r, g * c), blocks.dtype)
    for k in range(g):
        out = out.at[k * r:(k + 1) * r, k * c:(k + 1) * c].set(blocks[k])
    return out


def _pack_w_in(w_in, d):
    pw = nw = sw = cw = d // 4
    dh = NSA_DH
    o = np.cumsum([0, pw, nw, 6 * dh, 3 * NSA_HEADS, 2 * sw, 2 * cw])
    a, q, kv, gl, uv, ag = (w_in[:, o[k]:o[k + 1]] for k in range(6))
    gl_pad = jnp.zeros((d, LANES), w_in.dtype).at[:, :gl.shape[1]].set(gl)
    packed = jnp.concatenate([ag, uv, a, kv[:, :2 * dh], gl_pad, q * (dh ** -0.5), kv[:, 2 * dh:]], axis=1)
    return packed.astype(BF16)


def _pack_compress(pe_k, w1k, pe_v, w1v, w2k, w2v):
    dh = NSA_DH
    half = CMP_STRIDE
    hid = w1k.shape[1]

    def halves(pe, w1):
        w = w1.reshape(CMP_BLOCK, dh, hid)
        return (pe[:half], pe[half:]), (w[:half], w[half:])

    (pka, pkb), (wka, wkb) = halves(pe_k, w1k)
    (pva, pvb), (wva, wvb) = halves(pe_v, w1v)
    zeros = jnp.zeros((half, dh, hid), w1k.dtype)

    def interleave_w(wk, wv):
        top = jnp.concatenate([wk, zeros], axis=2)
        bot = jnp.concatenate([zeros, wv], axis=2)
        return jnp.concatenate([top, bot], axis=1).reshape(half * 2 * dh, 2 * hid).astype(BF16)

    def interleave_pe(pk, pv):
        return jnp.concatenate([pk, pv], axis=1).reshape(1, half * 2 * dh)

    w2_bd = _block_diag(jnp.stack([w2k, w2v])).astype(BF16)
    return (interleave_pe(pka, pva), interleave_pe(pkb, pvb), interleave_w(wka, wva), interleave_w(wkb, wvb), w2_bd)


def _importance_matrix(t):
    n_chunk = t // CMP_STRIDE
    per = CMP_BLOCK // CMP_STRIDE
    nc = n_chunk - per + 1
    ns = t // SEL_BLOCK
    ratio = SEL_BLOCK // CMP_STRIDE
    lead = per - 1
    m = np.zeros((n_chunk, ns), np.float32)
    for s in range(ns):
        for j in list(range(ratio * s, ratio * s + ratio)) + [ratio + p + ratio * s for p in range(lead)]:
            n = j - lead
            if 0 <= n < nc:
                m[n, s] += 1.0
    return jnp.asarray(m, BF16)


def kernel(x, mem, w_in, pool_w, pool_scale, cmp_pe_k, cmp_k_w1, cmp_k_w2, cmp_pe_v, cmp_v_w1, cmp_v_w2, sgu_ln_g, sgu_ln_b, sgu_w, sgu_b, conv_w, conv_b, conv_ln_g, conv_ln_b, conv_pw, w_out, ln1_g, ln1_b, xq_w, xkv_w, xo_w, ln2_g, ln2_b, ffn_w13, ffn_w2, router_w, exp_w13, exp_w2, ln3_g, ln3_b):
    b, t, d = x.shape
    depth = w_in.shape[0]
    alpha = float((2 * depth) ** 0.25)
    n = b * t
    m = mem.shape[1]
    sw = d // 4
    imp_w = _importance_matrix(t)
    row = lambda v: v.reshape(1, -1)
    x2 = x.reshape(n, d)
    mem2 = mem.reshape(b * m, d)
    exp_w13_all = exp_w13.astype(BF16).reshape((-1,) + exp_w13.shape[2:])
    exp_w2_all = exp_w2.astype(BF16).reshape((-1,) + exp_w2.shape[2:])
    for l in range(depth):
        ag, uv, pa, kvc, gl, q, kv = _in_proj(x2, _pack_w_in(w_in[l], d))
        pool_o = _pool_mixer(pa.reshape(b, t, -1), _block_diag(pool_w[l]).astype(BF16), row(pool_scale[l]))
        cmp_w = _pack_compress(cmp_pe_k[l], cmp_k_w1[l], cmp_pe_v[l], cmp_v_w1[l], cmp_k_w2[l], cmp_v_w2[l])
        kcv = _compress(kvc.reshape(b, t // CMP_STRIDE, CMP_STRIDE * 2 * NSA_DH), *cmp_w)
        nsa_o = _nsa_attention(q.reshape(b, t, -1), gl.reshape(b, t, -1), kcv, kv.reshape(b, t, -1), imp_w)
        bias_full = jnp.repeat(sgu_b[l].T, sw // SGU_GROUPS, axis=1)
        sgu_o = _sgu_mixer(uv, row(sgu_ln_g[l]), row(sgu_ln_b[l]), sgu_w[l], bias_full)
        conv_o = _conv_mixer(ag.reshape(b, t, -1), conv_w[l], row(conv_b[l]), row(conv_ln_g[l]),
                             row(conv_ln_b[l]), conv_pw[l].astype(BF16))
        parts = [pool_o.reshape(n, -1), nsa_o.reshape(n, -1), sgu_o, conv_o.reshape(n, -1)]
        x2 = _out_proj_ln(parts, x2, w_out[l].astype(BF16), row(ln1_g[l]), row(ln1_b[l]), alpha)
        kvm = _matmul(mem2, xkv_w[l].astype(BF16), BF16)
        x2 = _xattn_ln(x2.reshape(b, t, d), kvm.reshape(b, m, 2 * d), xq_w[l].astype(BF16),
                       xo_w[l].astype(BF16), row(ln2_g[l]), row(ln2_b[l]), alpha).reshape(n, d)
        if l % 2 == 0:
            x2 = _ffn_ln(x2, ffn_w13[l // 2].astype(BF16), ffn_w2[l // 2].astype(BF16),
                         row(ln3_g[l]), row(ln3_b[l]), alpha)
        else:
            x2 = _moe_ln(x2, router_w[l // 2], exp_w13_all, exp_w2_all, l // 2,
                         row(ln3_g[l]), row(ln3_b[l]), alpha)
    return x2.reshape(b, t, d)
```
